```python
import math
import jax, jax.numpy as jnp
from jax import lax
import numpy as np

D_MODEL = 1024
BATCH = 8
SEQ = 2048
DEPTH = 4

f32 = jnp.float32
N_A_LAYERS = DEPTH // 2

SSM_EXPAND = 2
D_INNER = SSM_EXPAND * D_MODEL
SSM_HEAD_DIM = 64
SSM_HEADS = D_INNER // SSM_HEAD_DIM
SSM_GROUPS = 4
SSM_STATE = 128
CONV_WIDTH = 4
SSM_CHUNK = 128
CONV_DIM = D_INNER + 2 * SSM_GROUPS * SSM_STATE
IN_PROJ_DIM = D_INNER + CONV_DIM + SSM_HEADS

ATTN_HEADS = 16
ATTN_HEAD_DIM = 64
ATTN_DIM = ATTN_HEADS * ATTN_HEAD_DIM
ROT_DIM = ATTN_HEAD_DIM // 4
ROPE_THETA = 500000.0
MOBA_BLOCK = 256
MOBA_TOPK = 3
MOBA_Q_CHUNK = 64

N_EXPERTS = 32
TOP_K = 4
D_EXPERT = D_MODEL
SWIGLU_LIMIT = 7.0
SWIGLU_ALPHA = 1.702
MOE_ROW_BLOCK = 512

LN_EPS = 1e-5
RMS_EPS = 1e-5
DEEPNORM_ALPHA = (2 * DEPTH) ** 0.25
DEEPNORM_BETA = (8 * DEPTH) ** -0.25

kernel_name = "yoco_mamba2_moba_moe_deepnorm"


def layer_norm(x, g, b):
    xf = x.astype(f32)
    mu = jnp.mean(xf, -1, keepdims=True)
    xc = xf - mu
    var = jnp.mean(xc * xc, -1, keepdims=True)
    return (xc * lax.rsqrt(var + LN_EPS) * g + b).astype(x.dtype)


def causal_depthwise_conv(u, w, b):
    c = u.shape[-1]
    out = lax.conv_general_dilated(u, w[:, None, :], window_strides=(1,),
                                   padding=[(CONV_WIDTH - 1, 0)],
                                   dimension_numbers=('NWC', 'WIO', 'NWC'),
                                   feature_group_count=c)
    return out + b


def gated_rmsnorm(y, z, g):
    h = y.astype(f32) * jax.nn.silu(z.astype(f32))
    hg = h.reshape(*h.shape[:-1], SSM_GROUPS, D_INNER // SSM_GROUPS)
    hg = hg * lax.rsqrt(jnp.mean(hg * hg, -1, keepdims=True) + RMS_EPS)
    return (hg.reshape(h.shape) * g).astype(y.dtype)


def ssd_chunked_scan(xs, dt, a, b_in, c_in):
    bsz, s, _, _ = xs.shape
    nc, L, G, R = s // SSM_CHUNK, SSM_CHUNK, SSM_GROUPS, SSM_HEADS // SSM_GROUPS
    x_dt = (xs.astype(f32) * dt[..., None]).reshape(bsz, nc, L, G, R, SSM_HEAD_DIM)
    a_dt = (dt * a).reshape(bsz, nc, L, G, R).transpose(0, 3, 4, 1, 2)
    bb = b_in.astype(f32).reshape(bsz, nc, L, G, SSM_STATE)
    cc = c_in.astype(f32).reshape(bsz, nc, L, G, SSM_STATE)
    a_cs = jnp.cumsum(a_dt, -1)
    tril = jnp.tril(jnp.ones((L, L), bool))
    decay = jnp.exp(jnp.where(tril, a_cs[..., :, None] - a_cs[..., None, :], -jnp.inf))
    cb = jnp.einsum('bclgn,bcsgn->bgcls', cc, bb)
    y_diag = jnp.einsum('bgcls,bgrcls,bcsgrp->bclgrp', cb, decay, x_dt)
    decay_to_end = jnp.exp(a_cs[..., -1:] - a_cs)
    chunk_states = jnp.einsum('bclgn,bgrcl,bclgrp->bcgrpn', bb, decay_to_end, x_dt)
    chunk_decay = jnp.exp(a_cs[..., -1])

    def step(state, inp):
        st, dec = inp
        return state * dec[..., None, None] + st, state

    init = jnp.zeros((bsz, G, R, SSM_HEAD_DIM, SSM_STATE), f32)
    _, prev_states = lax.scan(step, init, (jnp.moveaxis(chunk_states, 1, 0),
                                           jnp.moveaxis(chunk_decay, -1, 0)))
    y_off = jnp.einsum('bclgn,cbgrpn,bgrcl->bclgrp', cc, prev_states, jnp.exp(a_cs))
    return (y_diag + y_off).reshape(bsz, s, SSM_HEADS, SSM_HEAD_DIM)


def mamba2_mixer(x, w_in, conv_w, conv_b, dt_bias, a_log, d_skip, norm_g, w_out):
    bsz, s, _ = x.shape
    zxbcdt = x @ w_in
    z, xbc, dt = jnp.split(zxbcdt, [D_INNER, D_INNER + CONV_DIM], axis=-1)
    xbc = jax.nn.silu(causal_depthwise_conv(xbc, conv_w, conv_b))
    xs, b_in, c_in = jnp.split(xbc, [D_INNER, D_INNER + SSM_GROUPS * SSM_STATE], axis=-1)
    dt = jax.nn.softplus((dt + dt_bias).astype(f32))
    a = -jnp.exp(a_log.astype(f32))
    xs = xs.reshape(bsz, s, SSM_HEADS, SSM_HEAD_DIM)
    y = ssd_chunked_scan(xs, dt, a, b_in.reshape(bsz, s, SSM_GROUPS, SSM_STATE),
                         c_in.reshape(bsz, s, SSM_GROUPS, SSM_STATE))
    y = y + d_skip.astype(f32)[:, None] * xs.astype(f32)
    y = gated_rmsnorm(y.reshape(bsz, s, D_INNER).astype(x.dtype), z, norm_g)
    return y @ w_out


def rope_tables(s):
    inv_freq = ROPE_THETA ** (-jnp.arange(0, ROT_DIM, 2, dtype=f32) / ROT_DIM)
    ang = jnp.arange(s, dtype=f32)[:, None] * inv_freq[None, :]
    return jnp.cos(ang), jnp.sin(ang)


def apply_partial_rope(t, cos, sin):
    tf = t.astype(f32)
    half = ROT_DIM // 2
    t1, t2 = tf[..., :half], tf[..., half:ROT_DIM]
    c, s = cos[None, :, None, :], sin[None, :, None, :]
    return jnp.concatenate([t1 * c - t2 * s, t2 * c + t1 * s, tf[..., ROT_DIM:]], -1).astype(t.dtype)


def shared_kv(h, w_k, w_v, cos, sin):
    bsz, s, _ = h.shape
    nb = -(-s // MOBA_BLOCK)
    pad = nb * MOBA_BLOCK - s
    k = apply_partial_rope((h @ w_k).reshape(bsz, s, ATTN_HEADS, ATTN_HEAD_DIM), cos, sin)
    v = (h @ w_v).reshape(bsz, s, ATTN_HEADS, ATTN_HEAD_DIM)
    padw = ((0, 0), (0, pad), (0, 0), (0, 0))
    k_blk = jnp.pad(k, padw).reshape(bsz, nb, MOBA_BLOCK, ATTN_HEADS, ATTN_HEAD_DIM).transpose(0, 3, 1, 2, 4)
    v_blk = jnp.pad(v, padw).reshape(bsz, nb, MOBA_BLOCK, ATTN_HEADS, ATTN_HEAD_DIM).transpose(0, 3, 1, 2, 4)
    k_mean = jnp.mean(k_blk.astype(f32), axis=3)
    return k_blk, v_blk, k_mean


def moba_attention(h, w_q, w_o, k_blk, v_blk, k_mean, cos, sin):
    bsz, s, _ = h.shape
    nb = k_blk.shape[2]
    n_sel = min(MOBA_TOPK, nb)
    q = apply_partial_rope((h @ w_q).reshape(bsz, s, ATTN_HEADS, ATTN_HEAD_DIM), cos, sin).transpose(0, 2, 1, 3)
    q_blk = jnp.arange(s) // MOBA_BLOCK
    gate = jnp.einsum('bhtd,bhnd->bhtn', q.astype(f32), k_mean)
    past = jnp.arange(nb)[None, :] < q_blk[:, None]
    gate = jnp.where(past, gate, -jnp.inf)
    _, sel = lax.top_k(gate, n_sel)
    n_qc = s // MOBA_Q_CHUNK
    scale = ATTN_HEAD_DIM ** -0.5
    head_ix = jnp.arange(ATTN_HEADS)[:, None, None]

    def attend_chunk(ix):
        b = ix // n_qc
        t0 = (ix % n_qc) * MOBA_Q_CHUNK
        qc = lax.dynamic_slice_in_dim(q[b], t0, MOBA_Q_CHUNK, axis=1).astype(f32)
        sc = lax.dynamic_slice_in_dim(sel[b], t0, MOBA_Q_CHUNK, axis=1)
        kb, vb = k_blk[b], v_blk[b]
        k_sel = kb[head_ix, sc].astype(f32)
        v_sel = vb[head_ix, sc].astype(f32)
        cblk = t0 // MOBA_BLOCK
        k_own = lax.dynamic_index_in_dim(kb, cblk, axis=1, keepdims=False).astype(f32)
        v_own = lax.dynamic_index_in_dim(vb, cblk, axis=1, keepdims=False).astype(f32)
        s_sel = jnp.einsum('htd,htjsd->htjs', qc, k_sel) * scale
        s_sel = jnp.where((jnp.arange(n_sel) < cblk)[None, None, :, None], s_sel, -jnp.inf)
        s_own = jnp.einsum('htd,hsd->hts', qc, k_own) * scale
        t_abs = t0 + jnp.arange(MOBA_Q_CHUNK)
        s_abs = cblk * MOBA_BLOCK + jnp.arange(MOBA_BLOCK)
        s_own = jnp.where(s_abs[None, :] <= t_abs[:, None], s_own, -jnp.inf)
        scores = jnp.concatenate([s_sel.reshape(ATTN_HEADS, MOBA_Q_CHUNK, n_sel * MOBA_BLOCK), s_own], -1)
        p = jax.nn.softmax(scores, -1)
        p_sel = p[..., :n_sel * MOBA_BLOCK].reshape(ATTN_HEADS, MOBA_Q_CHUNK, n_sel, MOBA_BLOCK)
        p_own = p[..., n_sel * MOBA_BLOCK:]
        o = jnp.einsum('htjs,htjsd->htd', p_sel, v_sel) + jnp.einsum('hts,hsd->htd', p_own, v_own)
        return o.astype(h.dtype)

    out = lax.map(attend_chunk, jnp.arange(bsz * n_qc))
    out = out.reshape(bsz, n_qc, ATTN_HEADS, MOBA_Q_CHUNK, ATTN_HEAD_DIM).transpose(0, 1, 3, 2, 4)
    return out.reshape(bsz, s, ATTN_DIM) @ w_o


def moe_ffn(h, w_router, b_router, w_gu, b_gu, w_down, b_down):
    bsz, s, d = h.shape
    t = h.reshape(-1, d)
    n_tok = t.shape[0]
    n_assign = n_tok * TOP_K
    logits = (t @ w_router + b_router).astype(f32)
    top_val, top_idx = lax.top_k(logits, TOP_K)
    gate = jax.nn.softmax(top_val, axis=-1)
    flat_e = top_idx.reshape(-1)
    flat_tok = jnp.arange(n_assign) // TOP_K
    order = jnp.argsort(flat_e)
    e_sorted, tok_sorted = flat_e[order], flat_tok[order]
    g_sorted = gate.reshape(-1)[order]
    counts = jnp.bincount(flat_e, length=N_EXPERTS)
    starts = jnp.cumsum(counts) - counts
    padded = (counts + MOE_ROW_BLOCK - 1) // MOE_ROW_BLOCK * MOE_ROW_BLOCK
    pad_ends = jnp.cumsum(padded)
    pad_starts = pad_ends - padded
    dest = pad_starts[e_sorted] + (jnp.arange(n_assign) - starts[e_sorted])
    n_blocks = -(-n_assign // MOE_ROW_BLOCK) + N_EXPERTS
    n_rows = n_blocks * MOE_ROW_BLOCK
    buf = jnp.zeros((n_rows, d), t.dtype).at[dest].set(t[tok_sorted])
    blk_e = jnp.minimum(jnp.searchsorted(pad_ends, jnp.arange(n_blocks) * MOE_ROW_BLOCK, side='right'),
                        N_EXPERTS - 1)

    def expert_block(args):
        xb, e = args
        gu = xb @ w_gu[e] + b_gu[e]
        g_lin = jnp.minimum(gu[:, :D_EXPERT], SWIGLU_LIMIT)
        u_lin = jnp.clip(gu[:, D_EXPERT:], -SWIGLU_LIMIT, SWIGLU_LIMIT)
        act = (u_lin + 1.0) * (g_lin * jax.nn.sigmoid(g_lin * SWIGLU_ALPHA))
        return act @ w_down[e] + b_down[e]

    ybuf = lax.map(expert_block, (buf.reshape(n_blocks, MOE_ROW_BLOCK, d), blk_e)).reshape(n_rows, d)
    y = jnp.zeros((n_tok, d), f32).at[tok_sorted].add(g_sorted[:, None] * ybuf[dest].astype(f32))
    return y.astype(h.dtype).reshape(bsz, s, d)


def setup_inputs(seed: int = 0) -> dict:
    key = jax.random.key(seed)
    ks = jax.random.split(key, 24)
    n_a, n_b = N_A_LAYERS, DEPTH - N_A_LAYERS

    def nrm(k, shape, scale):
        return jax.random.normal(k, shape, f32) * scale

    dt0 = jnp.exp(jax.random.uniform(ks[4], (n_a, SSM_HEADS), f32, math.log(1e-3), math.log(1e-1)))
    return {
        "x": nrm(ks[0], (BATCH, SEQ, D_MODEL), 1.0),
        "ssm_w_in": nrm(ks[1], (n_a, D_MODEL, IN_PROJ_DIM), D_MODEL ** -0.5),
        "ssm_conv_w": nrm(ks[2], (n_a, CONV_WIDTH, CONV_DIM), CONV_WIDTH ** -0.5),
        "ssm_conv_b": nrm(ks[3], (n_a, CONV_DIM), 0.02),
        "ssm_dt_bias": dt0 + jnp.log(-jnp.expm1(-dt0)),
        "ssm_a_log": jnp.log(jax.random.uniform(ks[5], (n_a, SSM_HEADS), f32, 1.0, 16.0)),
        "ssm_d": 1.0 + nrm(ks[6], (n_a, SSM_HEADS), 0.01),
        "ssm_norm_g": 1.0 + nrm(ks[7], (n_a, D_INNER), 0.01),
        "ssm_w_out": nrm(ks[8], (n_a, D_INNER, D_MODEL), D_INNER ** -0.5 * DEEPNORM_BETA),
        "kv_w_k": nrm(ks[9], (D_MODEL, ATTN_DIM), D_MODEL ** -0.5),
        "kv_w_v": nrm(ks[10], (D_MODEL, ATTN_DIM), D_MODEL ** -0.5 * DEEPNORM_BETA),
        "attn_w_q": nrm(ks[11], (n_b, D_MODEL, ATTN_DIM), D_MODEL ** -0.5),
        "attn_w_o": nrm(ks[12], (n_b, ATTN_DIM, D_MODEL), ATTN_DIM ** -0.5 * DEEPNORM_BETA),
        "moe_w_router": nrm(ks[13], (DEPTH, D_MODEL, N_EXPERTS), D_MODEL ** -0.5),
        "moe_b_router": nrm(ks[14], (DEPTH, N_EXPERTS), 0.01),
        "moe_w_gate_up": nrm(ks[15], (DEPTH, N_EXPERTS, D_MODEL, 2 * D_EXPERT), D_MODEL ** -0.5),
        "moe_b_gate_up": nrm(ks[16], (DEPTH, N_EXPERTS, 2 * D_EXPERT), 0.01),
        "moe_w_down": nrm(ks[17], (DEPTH, N_EXPERTS, D_EXPERT, D_MODEL), D_EXPERT ** -0.5 * DEEPNORM_BETA),
        "moe_b_down": nrm(ks[18], (DEPTH, N_EXPERTS, D_MODEL), 0.01),
        "ln_mix_g": 1.0 + nrm(ks[19], (DEPTH, D_MODEL), 0.01),
        "ln_mix_b": nrm(ks[20], (DEPTH, D_MODEL), 0.01),
        "ln_ffn_g": 1.0 + nrm(ks[21], (DEPTH, D_MODEL), 0.01),
        "ln_ffn_b": nrm(ks[22], (DEPTH, D_MODEL), 0.01),
    }


def reference(x, ssm_w_in, ssm_conv_w, ssm_conv_b, ssm_dt_bias, ssm_a_log, ssm_d, ssm_norm_g, ssm_w_out,
              kv_w_k, kv_w_v, attn_w_q, attn_w_o, moe_w_router, moe_b_router, moe_w_gate_up, moe_b_gate_up,
              moe_w_down, moe_b_down, ln_mix_g, ln_mix_b, ln_ffn_g, ln_ffn_b):
    s = x.shape[1]
    cos, sin = rope_tables(s)
    h = x
    k_blk = v_blk = k_mean = None
    for l in range(DEPTH):
        if l < N_A_LAYERS:
            mix = mamba2_mixer(h, ssm_w_in[l], ssm_conv_w[l], ssm_conv_b[l], ssm_dt_bias[l], ssm_a_log[l],
                               ssm_d[l], ssm_norm_g[l], ssm_w_out[l])
        else:
            if l == N_A_LAYERS:
                k_blk, v_blk, k_mean = shared_kv(h, kv_w_k, kv_w_v, cos, sin)
            j = l - N_A_LAYERS
            mix = moba_attention(h, attn_w_q[j], attn_w_o[j], k_blk, v_blk, k_mean, cos, sin)
        h = layer_norm(DEEPNORM_ALPHA * h + mix, ln_mix_g[l], ln_mix_b[l])
        ffn = moe_ffn(h, moe_w_router[l], moe_b_router[l], moe_w_gate_up[l], moe_b_gate_up[l],
                      moe_w_down[l], moe_b_down[l])
        h = layer_norm(DEEPNORM_ALPHA * h + ffn, ln_ffn_g[l], ln_ffn_b[l])
    return h
```

```python
import functools

import jax
import jax.numpy as jnp
from jax import lax
from jax.experimental import pallas as pl
from jax.experimental.pallas import tpu as pltpu

f32, bf16, i32 = jnp.float32, jnp.bfloat16, jnp.int32

D_MODEL = 1024
DEPTH = 4
N_A_LAYERS = DEPTH // 2

D_INNER = 2048
SSM_HEAD_DIM = 64
SSM_HEADS = D_INNER // SSM_HEAD_DIM
SSM_GROUPS = 4
SSM_STATE = 128
CONV_WIDTH = 4
SSM_CHUNK = 128
BC_DIM = SSM_GROUPS * SSM_STATE
CONV_DIM = D_INNER + 2 * BC_DIM
GROUP_COLS = D_INNER // SSM_GROUPS

ATTN_HEADS = 16
ATTN_HEAD_DIM = 64
ATTN_DIM = ATTN_HEADS * ATTN_HEAD_DIM
ROT_DIM = ATTN_HEAD_DIM // 4
ROPE_THETA = 500000.0
MOBA_BLOCK = 256
MOBA_TOPK = 3

N_EXPERTS = 32
TOP_K = 4
D_EXPERT = D_MODEL
SWIGLU_LIMIT = 7.0
SWIGLU_ALPHA = 1.702

LN_EPS = 1e-5
RMS_EPS = 1e-5
DEEPNORM_ALPHA = (2 * DEPTH) ** 0.25

LANES = 128
SUBLANES = 8
VMEM_LIMIT = 52 * 1024 * 1024

ROW_TILE = 512
MOE_TM = 256
ROUTER_TILE = 512
COMBINE_TILE = 128

NEG_INF = float("-inf")


def _nt(a, b):
    return lax.dot_general(a, b, (((1,), (1,)), ((), ())), preferred_element_type=f32)


def _tn(a, b):
    return lax.dot_general(a, b, (((0,), (0,)), ((), ())), preferred_element_type=f32)


def _dot(a, b):
    return jnp.dot(a, b, preferred_element_type=f32)


def _split2(x):
    hi = x.astype(bf16)
    lo = (x - hi.astype(f32)).astype(bf16)
    return hi, lo


def _split3(x):
    hi = x.astype(bf16)
    r = x - hi.astype(f32)
    mid = r.astype(bf16)
    lo = (r - mid.astype(f32)).astype(bf16)
    return hi, mid, lo


def _sigmoid(x):
    return 1.0 / (1.0 + jnp.exp(-x))


def _layer_norm(v, g, b):
    mu = jnp.mean(v, axis=-1, keepdims=True)
    vc = v - mu
    var = jnp.mean(vc * vc, axis=-1, keepdims=True)
    return vc * lax.rsqrt(var + LN_EPS) * g + b


def _params(*sem):
    return pltpu.CompilerParams(dimension_semantics=sem, vmem_limit_bytes=VMEM_LIMIT)


def _in_proj_body(h_ref, wz_ref, wx_ref, wdt_ref, z_ref, xbc_ref, dt_ref):
    x = h_ref[...].astype(bf16)
    for c in range(D_INNER // 1024):
        cs = slice(c * 1024, (c + 1) * 1024)
        z_ref[:, cs] = _dot(x, wz_ref[:, cs]).astype(bf16)
    for c in range(CONV_DIM // 1024):
        cs = slice(c * 1024, (c + 1) * 1024)
        xbc_ref[:, cs] = _dot(x, wx_ref[:, cs]).astype(bf16)
    dt_ref[...] = _dot(x, wdt_ref[...])


def _in_proj(h, wz, wx, wdt):
    n = h.shape[0]
    tm = ROW_TILE
    const = lambda i: (0, 0)
    return pl.pallas_call(
        _in_proj_body,
        grid=(n // tm,),
        in_specs=[pl.BlockSpec((tm, D_MODEL), lambda i: (i, 0)),
                  pl.BlockSpec((D_MODEL, D_INNER), const),
                  pl.BlockSpec((D_MODEL, CONV_DIM), const),
                  pl.BlockSpec((D_MODEL, LANES), const)],
        out_specs=[pl.BlockSpec((tm, D_INNER), lambda i: (i, 0)),
                   pl.BlockSpec((tm, CONV_DIM), lambda i: (i, 0)),
                   pl.BlockSpec((tm, LANES), lambda i: (i, 0))],
        out_shape=[jax.ShapeDtypeStruct((n, D_INNER), bf16),
                   jax.ShapeDtypeStruct((n, CONV_DIM), bf16),
                   jax.ShapeDtypeStruct((n, LANES), f32)],
        compiler_params=_params("parallel"),
        name="ssm_in_proj",
    )(h, wz, wx, wdt)


_EXT_PAD = SUBLANES


def _ssd_body(z_ref, xbc_ref, dt_ref, cw_ref, cb_ref, dtb_ref, alog_ref, dexp_ref, ng_ref, e_ref, tril_ref,
              y_ref, ext_scr, act_scr, st_scr, y_scr, ex_scr):
    c = pl.program_id(1)
    L = SSM_CHUNK

    @pl.when(c == 0)
    def _():
        ext_scr[0:_EXT_PAD, :] = jnp.zeros((_EXT_PAD, CONV_DIM), f32)
        st_scr[...] = jnp.zeros_like(st_scr)

    ext_scr[_EXT_PAD:_EXT_PAD + L, :] = xbc_ref[...].astype(f32)
    for cc in range(CONV_DIM // 512):
        cs = slice(cc * 512, (cc + 1) * 512)
        conv = cb_ref[:, cs] + cw_ref[CONV_WIDTH - 1:CONV_WIDTH, cs] * ext_scr[_EXT_PAD:_EXT_PAD + L, cs]
        for j in range(1, CONV_WIDTH):
            conv = conv + cw_ref[CONV_WIDTH - 1 - j:CONV_WIDTH - j, cs] * ext_scr[_EXT_PAD - j:_EXT_PAD - j + L, cs]
        act_scr[:, cs] = conv * _sigmoid(conv)
    ext_scr[0:_EXT_PAD, :] = ext_scr[L:L + _EXT_PAD, :]

    dtv = dt_ref[...] + dtb_ref[...]
    dt = jnp.maximum(dtv, 0.0) + jnp.log1p(jnp.exp(-jnp.abs(dtv)))
    a = -jnp.exp(alog_ref[...])
    adt = dt * a
    tril = tril_ref[...]
    a1, a2, a3 = _split3(adt)
    acs = _dot(tril, a1) + _dot(tril, a2) + _dot(tril, a3)
    acs_t = acs.T
    dt_t = dt.T
    a_last = acs[L - 1:L, :]
    dte = jnp.exp(a_last - acs)
    ea = jnp.exp(acs)
    cd = jnp.exp(a_last)
    stack = jnp.concatenate([dt * dte, ea, jnp.broadcast_to(cd, (SUBLANES, LANES))], axis=0)
    s_hi, s_lo = _split2(stack)
    ex_scr[...] = _dot(s_hi, e_ref[...]) + _dot(s_lo, e_ref[...])

    row = lax.broadcasted_iota(i32, (L, L), 0)
    col = lax.broadcasted_iota(i32, (L, L), 1)
    lower = row >= col
    lane = lax.broadcasted_iota(i32, (1, LANES), 1)
    first_head = lane < SSM_HEAD_DIM

    for g in range(SSM_GROUPS):
        gs = slice(g * GROUP_COLS, (g + 1) * GROUP_COLS)
        b_g = act_scr[:, D_INNER + g * SSM_STATE:D_INNER + (g + 1) * SSM_STATE].astype(bf16)
        c_g = act_scr[:, D_INNER + BC_DIM + g * SSM_STATE:D_INNER + BC_DIM + (g + 1) * SSM_STATE].astype(bf16)
        cb = _nt(c_g, b_g)
        st_old = st_scr[:, gs]
        xs_g = act_scr[:, gs]
        y_scr[:, gs] = _dot(c_g, st_old.astype(bf16)) * ex_scr[L:2 * L, gs] + dexp_ref[:, gs] * xs_g
        xw = (xs_g * ex_scr[0:L, gs]).astype(bf16)
        st_scr[:, gs] = st_old * ex_scr[2 * L:2 * L + 1, gs] + _tn(b_g, xw)
        for j in range(GROUP_COLS // LANES):
            h0 = g * (GROUP_COLS // SSM_HEAD_DIM) + 2 * j
            ms = []
            for h in (h0, h0 + 1):
                diff = acs[:, h:h + 1] - acs_t[h:h + 1, :]
                decay = jnp.exp(jnp.where(lower, diff, NEG_INF))
                ms.append((cb * decay * dt_t[h:h + 1, :]).astype(bf16))
            lhs = jnp.concatenate(ms, axis=1)
            ps = slice(g * GROUP_COLS + j * LANES, g * GROUP_COLS + (j + 1) * LANES)
            xp = act_scr[:, ps].astype(bf16)
            zero = jnp.zeros_like(xp)
            rhs = jnp.concatenate([jnp.where(first_head, xp, zero), jnp.where(first_head, zero, xp)], axis=0)
            y_scr[:, ps] += _dot(lhs, rhs)

    for g in range(SSM_GROUPS):
        gs = slice(g * GROUP_COLS, (g + 1) * GROUP_COLS)
        zf = z_ref[:, gs].astype(f32)
        hg = y_scr[:, gs] * (zf * _sigmoid(zf))
        ms = jnp.mean(hg * hg, axis=-1, keepdims=True)
        y_ref[:, gs] = (hg * lax.rsqrt(ms + RMS_EPS) * ng_ref[:, gs]).astype(bf16)


def _ssd(z, xbc, dt, conv_w, conv_b, dt_bias, a_log, d_skip, norm_g, bsz, seq):
    n = bsz * seq
    L = SSM_CHUNK
    nc = seq // L
    pad = LANES - SSM_HEADS
    dtb = jnp.pad(dt_bias.astype(f32), (0, pad)).reshape(1, LANES)
    alog = jnp.pad(a_log.astype(f32), (0, pad)).reshape(1, LANES)
    dexp = jnp.repeat(d_skip.astype(f32), SSM_HEAD_DIM).reshape(1, D_INNER)
    head_of_col = jnp.arange(D_INNER, dtype=i32) // SSM_HEAD_DIM
    expand = (jnp.arange(LANES, dtype=i32)[:, None] == head_of_col[None, :]).astype(bf16)
    tril = jnp.tril(jnp.ones((L, L), f32)).astype(bf16)
    tok = lambda b, c: (b * nc + c, 0)
    const = lambda b, c: (0, 0)
    ex_rows = 2 * L + SUBLANES
    return pl.pallas_call(
        _ssd_body,
        grid=(bsz, nc),
        in_specs=[pl.BlockSpec((L, D_INNER), tok),
                  pl.BlockSpec((L, CONV_DIM), tok),
                  pl.BlockSpec((L, LANES), tok),
                  pl.BlockSpec((CONV_WIDTH, CONV_DIM), const),
                  pl.BlockSpec((1, CONV_DIM), const),
                  pl.BlockSpec((1, LANES), const),
                  pl.BlockSpec((1, LANES), const),
                  pl.BlockSpec((1, D_INNER), const),
                  pl.BlockSpec((1, D_INNER), const),
                  pl.BlockSpec((LANES, D_INNER), const),
                  pl.BlockSpec((L, L), const)],
        out_specs=pl.BlockSpec((L, D_INNER), tok),
        out_shape=jax.ShapeDtypeStruct((n, D_INNER), bf16),
        scratch_shapes=[pltpu.VMEM((L + 2 * _EXT_PAD, CONV_DIM), f32),
                        pltpu.VMEM((L, CONV_DIM), f32),
                        pltpu.VMEM((SSM_STATE, D_INNER), f32),
                        pltpu.VMEM((L, D_INNER), f32),
                        pltpu.VMEM((ex_rows, D_INNER), f32)],
        compiler_params=_params("parallel", "arbitrary"),
        name="ssm_ssd",
    )(z, xbc, dt, conv_w.astype(f32), conv_b.astype(f32).reshape(1, CONV_DIM), dtb, alog, dexp,
      norm_g.astype(f32).reshape(1, D_INNER), expand, tril)


def _mm_res_ln_body(y_ref, w_ref, h_ref, g_ref, b_ref, o_ref):
    mix = _dot(y_ref[...], w_ref[...])
    o_ref[...] = _layer_norm(DEEPNORM_ALPHA * h_ref[...] + mix, g_ref[...], b_ref[...])


def _mm_res_ln(y, w, h, g, b):
    n, k = y.shape
    tm = ROW_TILE
    const = lambda i: (0, 0)
    return pl.pallas_call(
        _mm_res_ln_body,
        grid=(n // tm,),
        in_specs=[pl.BlockSpec((tm, k), lambda i: (i, 0)),
                  pl.BlockSpec((k, D_MODEL), const),
                  pl.BlockSpec((tm, D_MODEL), lambda i: (i, 0)),
                  pl.BlockSpec((1, D_MODEL), const),
                  pl.BlockSpec((1, D_MODEL), const)],
        out_specs=pl.BlockSpec((tm, D_MODEL), lambda i: (i, 0)),
        out_shape=jax.ShapeDtypeStruct((n, D_MODEL), f32),
        compiler_params=_params("parallel"),
        name="out_proj_ln",
    )(y, w, h, g.astype(f32).reshape(1, D_MODEL), b.astype(f32).reshape(1, D_MODEL))


def _rope_tables(seq):
    half = ROT_DIM // 2
    inv_freq = ROPE_THETA ** (-jnp.arange(0, ROT_DIM, 2, dtype=f32) / ROT_DIM)
    ang = jnp.arange(seq, dtype=f32)[:, None] * inv_freq[None, :]
    cos, sin = jnp.cos(ang), jnp.sin(ang)
    j = jnp.arange(LANES) % ATTN_HEAD_DIM
    idx = j % half
    first = (j < half)[None, :]
    second = ((j >= half) & (j < ROT_DIM))[None, :]
    cos_t = jnp.where(first | second, cos[:, idx], 1.0)
    sin_a = jnp.where(first, -sin[:, idx], 0.0)
    sin_b = jnp.where(second, sin[:, idx], 0.0)
    return cos_t, sin_a, sin_b


def _rope_cols(x, cos_t, sin_a, sin_b):
    half = ROT_DIM // 2
    return x * cos_t + pltpu.roll(x, LANES - half, axis=1) * sin_a + pltpu.roll(x, half, axis=1) * sin_b


def _q_proj_body(h_ref, w_ref, cos_ref, sa_ref, sb_ref, q_ref):
    x = h_ref[...].astype(bf16)
    scale = ATTN_HEAD_DIM ** -0.5
    for c in range(ATTN_DIM // LANES):
        cs = slice(c * LANES, (c + 1) * LANES)
        q = _rope_cols(_dot(x, w_ref[:, cs]), cos_ref[...], sa_ref[...], sb_ref[...])
        q_ref[:, cs] = (q * scale).astype(bf16)


def _q_proj(h, w, tables, seq):
    n = h.shape[0]
    tm = ROW_TILE
    per_seq = seq // tm
    const = lambda i: (0, 0)
    pos = lambda i: (i % per_seq, 0)
    return pl.pallas_call(
        _q_proj_body,
        grid=(n // tm,),
        in_specs=[pl.BlockSpec((tm, D_MODEL), lambda i: (i, 0)),
                  pl.BlockSpec((D_MODEL, ATTN_DIM), const),
                  pl.BlockSpec((tm, LANES), pos), pl.BlockSpec((tm, LANES), pos), pl.BlockSpec((tm, LANES), pos)],
        out_specs=pl.BlockSpec((tm, ATTN_DIM), lambda i: (i, 0)),
        out_shape=jax.ShapeDtypeStruct((n, ATTN_DIM), bf16),
        compiler_params=_params("parallel"),
        name="attn_q_proj",
    )(h, w, *tables)


def _kv_proj_body(h_ref, wk_ref, wv_ref, cos_ref, sa_ref, sb_ref, k_ref, v_ref, km_ref):
    x = h_ref[...].astype(bf16)
    for c in range(ATTN_DIM // LANES):
        cs = slice(c * LANES, (c + 1) * LANES)
        k = _rope_cols(_dot(x, wk_ref[:, cs]), cos_ref[...], sa_ref[...], sb_ref[...])
        k_ref[:, cs] = k.astype(bf16)
        km_ref[0, :, cs] = jnp.mean(k, axis=0, keepdims=True)
        v_ref[:, cs] = _dot(x, wv_ref[:, cs]).astype(bf16)


def _kv_proj(h, wk, wv, tables, seq):
    n = h.shape[0]
    tm = MOBA_BLOCK
    per_seq = seq // tm
    const = lambda i: (0, 0)
    pos = lambda i: (i % per_seq, 0)
    return pl.pallas_call(
        _kv_proj_body,
        grid=(n // tm,),
        in_specs=[pl.BlockSpec((tm, D_MODEL), lambda i: (i, 0)),
                  pl.BlockSpec((D_MODEL, ATTN_DIM), const),
                  pl.BlockSpec((D_MODEL, ATTN_DIM), const),
                  pl.BlockSpec((tm, LANES), pos), pl.BlockSpec((tm, LANES), pos), pl.BlockSpec((tm, LANES), pos)],
        out_specs=[pl.BlockSpec((tm, ATTN_DIM), lambda i: (i, 0)),
                   pl.BlockSpec((tm, ATTN_DIM), lambda i: (i, 0)),
                   pl.BlockSpec((1, 1, ATTN_DIM), lambda i: (i, 0, 0))],
        out_shape=[jax.ShapeDtypeStruct((n, ATTN_DIM), bf16),
                   jax.ShapeDtypeStruct((n, ATTN_DIM), bf16),
                   jax.ShapeDtypeStruct((n // tm, 1, ATTN_DIM), f32)],
        compiler_params=_params("parallel"),
        name="attn_kv_proj",
    )(h, wk, wv, *tables)


def _moba_body(q_ref, k_ref, v_ref, km_ref, o_ref, *, nb):
    qb = pl.program_id(2)
    blk = MOBA_BLOCK
    q = q_ref[...]
    lane = lax.broadcasted_iota(i32, (1, LANES), 1)
    first_head = lane < ATTN_HEAD_DIM
    row = lax.broadcasted_iota(i32, (blk, blk), 0)
    col = lax.broadcasted_iota(i32, (blk, blk), 1)
    causal = col <= row
    eye = (row == col).astype(bf16)
    blk_id = lax.broadcasted_iota(i32, (nb, blk), 0)
    blk_lane = lax.broadcasted_iota(i32, (1, nb), 1)
    km_hi, km_lo = _split2(km_ref[...])
    past = blk_id < qb
    own = pl.multiple_of(qb * blk, blk)
    outs = []
    for hd in range(2):
        keep = first_head if hd == 0 else jnp.logical_not(first_head)
        qm = jnp.where(keep, q, jnp.zeros_like(q))
        gate = jnp.where(past, _nt(km_hi, qm) + _nt(km_lo, qm), NEG_INF)
        rank = jnp.zeros((nb, blk), f32)
        for j2 in range(nb):
            gj = gate[j2:j2 + 1, :]
            beats = (gj > gate) | ((gj == gate) & (blk_id > j2))
            rank = rank + beats.astype(f32)
        sel_t = (past & (rank < MOBA_TOPK)).astype(bf16)
        sel = _nt(eye, sel_t)

        s = jnp.where(causal, _nt(qm, k_ref[pl.ds(own, blk), :]), NEG_INF)
        m = jnp.max(s, axis=-1, keepdims=True)
        p = jnp.exp(s - m)
        l = jnp.sum(p, axis=-1, keepdims=True)
        acc = _dot(p.astype(bf16), v_ref[pl.ds(own, blk), :])

        def past_block(j, carry, qm=qm, sel=sel):
            m, l, acc = carry
            off = pl.multiple_of(j * blk, blk)
            picked = jnp.sum(jnp.where(blk_lane == j, sel, 0.0), axis=-1, keepdims=True) > 0.5
            s = jnp.where(picked, _nt(qm, k_ref[pl.ds(off, blk), :]), NEG_INF)
            m_new = jnp.maximum(m, jnp.max(s, axis=-1, keepdims=True))
            alpha = jnp.exp(m - m_new)
            p = jnp.exp(s - m_new)
            l = alpha * l + jnp.sum(p, axis=-1, keepdims=True)
            acc = alpha * acc + _dot(p.astype(bf16), v_ref[pl.ds(off, blk), :])
            return m_new, l, acc

        m, l, acc = lax.fori_loop(0, qb, past_block, (m, l, acc))
        outs.append(acc / l)
    o_ref[...] = jnp.where(first_head, outs[0], outs[1]).astype(bf16)


def _moba(q, k, v, kmean, bsz, seq):
    n = bsz * seq
    nb = seq // MOBA_BLOCK
    n_pairs = ATTN_DIM // LANES
    return pl.pallas_call(
        functools.partial(_moba_body, nb=nb),
        grid=(bsz, n_pairs, nb),
        in_specs=[pl.BlockSpec((MOBA_BLOCK, LANES), lambda b, hp, qb: (b * nb + qb, hp)),
                  pl.BlockSpec((seq, LANES), lambda b, hp, qb: (b, hp)),
                  pl.BlockSpec((seq, LANES), lambda b, hp, qb: (b, hp)),
                  pl.BlockSpec((None, nb, LANES), lambda b, hp, qb: (b, 0, hp))],
        out_specs=pl.BlockSpec((MOBA_BLOCK, LANES), lambda b, hp, qb: (b * nb + qb, hp)),
        out_shape=jax.ShapeDtypeStruct((n, ATTN_DIM), bf16),
        compiler_params=_params("parallel", "parallel", "arbitrary"),
        name="moba_attention",
    )(q, k, v, kmean)


def _router_body(h_ref, wt_ref, b_ref, u_ref, idx_ref, gate_ref, rank_ref, cnt_ref, cnt_scr):
    i = pl.program_id(0)

    @pl.when(i == 0)
    def _():
        cnt_scr[...] = jnp.zeros_like(cnt_scr)

    tr = h_ref.shape[0]
    x_hi, x_lo = _split2(h_ref[...])
    w_hi, w_lo = _split2(wt_ref[...])
    logits = _nt(w_hi, x_hi) + _nt(w_hi, x_lo) + _nt(w_lo, x_hi) + b_ref[...]
    e_id = lax.broadcasted_iota(i32, (N_EXPERTS, tr), 0)
    cur = logits
    vals, idxs = [], []
    for _ in range(TOP_K):
        m = jnp.max(cur, axis=0, keepdims=True)
        idx = jnp.min(jnp.where(cur == m, e_id, N_EXPERTS), axis=0, keepdims=True)
        vals.append(m)
        idxs.append(idx)
        cur = jnp.where(e_id == idx, NEG_INF, cur)
    exps = [jnp.exp(v - vals[0]) for v in vals]
    denom = exps[0]
    for e in exps[1:]:
        denom = denom + e
    onehot = jnp.zeros((N_EXPERTS, tr), f32)
    for idx in idxs:
        onehot = onehot + (e_id == idx).astype(f32)
    base = cnt_scr[:, 0:1]
    before = _dot(onehot.astype(bf16), u_ref[...]) + base
    gate_ref[...] = jnp.zeros_like(gate_ref)
    for r in range(TOP_K):
        idx_ref[r:r + 1, :] = idxs[r]
        gate_ref[r:r + 1, :] = exps[r] / denom
        rank_ref[r:r + 1, :] = jnp.sum(jnp.where(e_id == idxs[r], before, 0.0), axis=0, keepdims=True).astype(i32)
    total = base + jnp.sum(onehot, axis=1, keepdims=True)
    cnt_scr[...] = jnp.broadcast_to(total, cnt_scr.shape)
    cnt_ref[...] = jnp.broadcast_to(total, cnt_ref.shape).astype(i32)


def _router(h, w_router, b_router):
    n = h.shape[0]
    tr = ROUTER_TILE
    wt = w_router.astype(f32).T
    b = b_router.astype(f32).reshape(N_EXPERTS, 1)
    strict_upper = (jnp.arange(tr)[:, None] < jnp.arange(tr)[None, :]).astype(bf16)
    const = lambda i: (0, 0)
    tok = lambda i: (0, i)
    return pl.pallas_call(
        _router_body,
        grid=(n // tr,),
        in_specs=[pl.BlockSpec((tr, D_MODEL), lambda i: (i, 0)),
                  pl.BlockSpec((N_EXPERTS, D_MODEL), const),
                  pl.BlockSpec((N_EXPERTS, 1), const),
                  pl.BlockSpec((tr, tr), const)],
        out_specs=[pl.BlockSpec((TOP_K, tr), tok),
                   pl.BlockSpec((SUBLANES, tr), tok),
                   pl.BlockSpec((TOP_K, tr), tok),
                   pl.BlockSpec((N_EXPERTS, LANES), const)],
        out_shape=[jax.ShapeDtypeStruct((TOP_K, n), i32),
                   jax.ShapeDtypeStruct((SUBLANES, n), f32),
                   jax.ShapeDtypeStruct((TOP_K, n), i32),
                   jax.ShapeDtypeStruct((N_EXPERTS, LANES), i32)],
        scratch_shapes=[pltpu.VMEM((N_EXPERTS, LANES), f32)],
        compiler_params=_params("arbitrary"),
        name="moe_router",
    )(h, wt, b, strict_upper)


def _gather_rows(idx_smem, src_hbm, dst_vmem, sem, n_rows, idx_base=0):
    def issue(r, carry):
        tok = idx_smem[idx_base + r]
        pltpu.make_async_copy(src_hbm.at[pl.ds(tok, 1)], dst_vmem.at[pl.ds(r, 1)], sem).start()
        return carry
    lax.fori_loop(0, n_rows, issue, 0)


def _wait_rows(src_hbm, dst_vmem, sem, n_rows):
    pltpu.make_async_copy(src_hbm.at[pl.ds(0, n_rows)], dst_vmem, sem).wait()


def _expert_body(blk_e_ref, nact_ref, tok_hbm, h_hbm, wgu_ref, bgu_ref, wd_ref, bd_ref, y_ref,
                 idx_smem, x_buf, idx_sem, row_sem):
    i = pl.program_id(0)
    tm = x_buf.shape[0]

    @pl.when(i < nact_ref[0])
    def _():
        cp = pltpu.make_async_copy(tok_hbm.at[pl.ds(pl.multiple_of(i * tm, tm), tm)], idx_smem, idx_sem)
        cp.start()
        cp.wait()
        _gather_rows(idx_smem, h_hbm, x_buf, row_sem, tm)
        _wait_rows(h_hbm, x_buf, row_sem, tm)
        x = x_buf[...].astype(bf16)
        gu = _dot(x, wgu_ref[...]) + bgu_ref[...]
        g_lin = jnp.minimum(gu[:, :D_EXPERT], SWIGLU_LIMIT)
        u_lin = jnp.clip(gu[:, D_EXPERT:], -SWIGLU_LIMIT, SWIGLU_LIMIT)
        act = (u_lin + 1.0) * (g_lin * _sigmoid(g_lin * SWIGLU_ALPHA))
        y_ref[...] = _dot(act.astype(bf16), wd_ref[...]) + bd_ref[...]

    @pl.when(i >= nact_ref[0])
    def _():
        y_ref[...] = jnp.zeros_like(y_ref)


def _expert_ffn(h, row_tok, blk_e, nact, w_gu, b_gu, w_down, b_down, n_blocks):
    tm = MOE_TM
    e_map = lambda i, be, na: (be[i], 0, 0)
    out_map = lambda i, be, na: (i, 0)
    return pl.pallas_call(
        _expert_body,
        grid_spec=pltpu.PrefetchScalarGridSpec(
            num_scalar_prefetch=2,
            grid=(n_blocks,),
            in_specs=[pl.BlockSpec(memory_space=pl.ANY),
                      pl.BlockSpec(memory_space=pl.ANY),
                      pl.BlockSpec((None, D_MODEL, 2 * D_EXPERT), e_map),
                      pl.BlockSpec((None, 1, 2 * D_EXPERT), e_map),
                      pl.BlockSpec((None, D_EXPERT, D_MODEL), e_map),
                      pl.BlockSpec((None, 1, D_MODEL), e_map)],
            out_specs=pl.BlockSpec((tm, D_MODEL), out_map),
            scratch_shapes=[pltpu.SMEM((tm,), i32),
                            pltpu.VMEM((tm, D_MODEL), f32),
                            pltpu.SemaphoreType.DMA(()),
                            pltpu.SemaphoreType.DMA(())]),
        out_shape=jax.ShapeDtypeStruct((n_blocks * tm, D_MODEL), f32),
        compiler_params=_params("arbitrary"),
        name="moe_experts",
    )(blk_e, nact, row_tok, h, w_gu, b_gu, w_down, b_down)


def _combine_body(dest_hbm, ybuf_hbm, gate_ref, h_ref, g_ref, b_ref, o_ref, idx_smem, rows, idx_sem, row_sem):
    i = pl.program_id(0)
    tc = h_ref.shape[0]
    n_idx = TOP_K * tc
    cp = pltpu.make_async_copy(dest_hbm.at[pl.ds(pl.multiple_of(i * n_idx, n_idx), n_idx)], idx_smem, idx_sem)
    cp.start()
    cp.wait()
    for k in range(TOP_K):
        _gather_rows(idx_smem, ybuf_hbm, rows.at[k], row_sem, tc, idx_base=k * tc)
    for k in range(TOP_K):
        _wait_rows(ybuf_hbm, rows.at[k], row_sem, tc)
    gates = jnp.concatenate([gate_ref[...], jnp.zeros((tc - SUBLANES, tc), f32)], axis=0).T
    moe = rows[0] * gates[:, 0:1]
    for k in range(1, TOP_K):
        moe = moe + rows[k] * gates[:, k:k + 1]
    o_ref[...] = _layer_norm(DEEPNORM_ALPHA * h_ref[...] + moe, g_ref[...], b_ref[...])


def _combine_ln(dest_tiles, ybuf, gates, h, g, b):
    n = h.shape[0]
    tc = COMBINE_TILE
    const = lambda i: (0, 0)
    return pl.pallas_call(
        _combine_body,
        grid=(n // tc,),
        in_specs=[pl.BlockSpec(memory_space=pl.ANY),
                  pl.BlockSpec(memory_space=pl.ANY),
                  pl.BlockSpec((SUBLANES, tc), lambda i: (0, i)),
                  pl.BlockSpec((tc, D_MODEL), lambda i: (i, 0)),
                  pl.BlockSpec((1, D_MODEL), const),
                  pl.BlockSpec((1, D_MODEL), const)],
        out_specs=pl.BlockSpec((tc, D_MODEL), lambda i: (i, 0)),
        out_shape=jax.ShapeDtypeStruct((n, D_MODEL), f32),
        scratch_shapes=[pltpu.SMEM((TOP_K * tc,), i32),
                        pltpu.VMEM((TOP_K, tc, D_MODEL), f32),
                        pltpu.SemaphoreType.DMA(()),
                        pltpu.SemaphoreType.DMA(())],
        compiler_params=_params("arbitrary"),
        name="moe_combine_ln",
    )(dest_tiles, ybuf, gates, h, g.astype(f32).reshape(1, D_MODEL), b.astype(f32).reshape(1, D_MODEL))


def _moe_layer(h, w_router, b_router, w_gu, b_gu, w_down, b_down, ln_g, ln_b):
    n = h.shape[0]
    tm = MOE_TM
    n_blocks = (n * TOP_K) // tm + N_EXPERTS
    idx, gates, rank, counts = _router(h, w_router, b_router)
    counts = counts[:, 0]
    blocks_per_e = (counts + tm - 1) // tm
    blk_end = jnp.cumsum(blocks_per_e)
    blk_start = blk_end - blocks_per_e
    nact = blk_end[-1:].astype(i32)
    dest = blk_start[idx] * tm + rank
    blk_ids = jnp.minimum(jnp.arange(n_blocks, dtype=i32), nact[0] - 1)
    blk_e = jnp.minimum(jnp.searchsorted(blk_end, blk_ids, side="right"), N_EXPERTS - 1).astype(i32)
    tok_ids = jnp.broadcast_to(jnp.arange(n, dtype=i32)[None, :], (TOP_K, n))
    row_tok = jnp.zeros((n_blocks * tm,), i32).at[dest.reshape(-1)].set(tok_ids.reshape(-1), unique_indices=True)
    ybuf = _expert_ffn(h, row_tok, blk_e, nact, w_gu, b_gu.astype(f32).reshape(N_EXPERTS, 1, 2 * D_EXPERT),
                       w_down, b_down.astype(f32).reshape(N_EXPERTS, 1, D_MODEL), n_blocks)
    tc = COMBINE_TILE
    dest_tiles = dest.reshape(TOP_K, n // tc, tc).transpose(1, 0, 2).reshape(-1)
    return _combine_ln(dest_tiles, ybuf, gates, h, ln_g, ln_b)


def _mamba_layer(h, bsz, seq, w_in, conv_w, conv_b, dt_bias, a_log, d_skip, norm_g, w_out, ln_g, ln_b):
    wz = w_in[:, :D_INNER].astype(bf16)
    wx = w_in[:, D_INNER:D_INNER + CONV_DIM].astype(bf16)
    wdt = jnp.pad(w_in[:, D_INNER + CONV_DIM:], ((0, 0), (0, LANES - SSM_HEADS))).astype(bf16)
    z, xbc, dt = _in_proj(h, wz, wx, wdt)
    y = _ssd(z, xbc, dt, conv_w, conv_b, dt_bias, a_log, d_skip, norm_g, bsz, seq)
    return _mm_res_ln(y, w_out.astype(bf16), h, ln_g, ln_b)


def _moba_layer(h, bsz, seq, w_q, w_o, k, v, kmean, tables, ln_g, ln_b):
    q = _q_proj(h, w_q.astype(bf16), tables, seq)
    o = _moba(q, k, v, kmean, bsz, seq)
    return _mm_res_ln(o, w_o.astype(bf16), h, ln_g, ln_b)


def kernel(x, ssm_w_in, ssm_conv_w, ssm_conv_b, ssm_dt_bias, ssm_a_log, ssm_d, ssm_norm_g, ssm_w_out,
           kv_w_k, kv_w_v, attn_w_q, attn_w_o, moe_w_router, moe_b_router, moe_w_gate_up, moe_b_gate_up,
           moe_w_down, moe_b_down, ln_mix_g, ln_mix_b, ln_ffn_g, ln_ffn_b):
    bsz, seq, _ = x.shape
    h = x.reshape(bsz * seq, D_MODEL).astype(f32)
    tables = _rope_tables(seq)
    k = v = kmean = None
    for l in range(DEPTH):
        if l < N_A_LAYERS:
            h = _mamba_layer(h, bsz, seq, ssm_w_in[l], ssm_conv_w[l], ssm_conv_b[l], ssm_dt_bias[l], ssm_a_log[l],
                             ssm_d[l], ssm_norm_g[l], ssm_w_out[l], ln_mix_g[l], ln_mix_b[l])
        else:
            if l == N_A_LAYERS:
                k, v, kmean = _kv_proj(h, kv_w_k.astype(bf16), kv_w_v.astype(bf16), tables, seq)
                kmean = kmean.reshape(bsz, seq // MOBA_BLOCK, ATTN_DIM)
            j = l - N_A_LAYERS
            h = _moba_layer(h, bsz, seq, attn_w_q[j], attn_w_o[j], k, v, kmean, tables, ln_mix_g[l], ln_mix_b[l])
        h = _moe_layer(h, moe_w_router[l], moe_b_router[l], moe_w_gate_up[l].astype(bf16), moe_b_gate_up[l],
                       moe_w_down[l].astype(bf16), moe_b_down[l], ln_ffn_g[l], ln_ffn_b[l])
    return h.reshape(bsz, seq, D_MODEL).astype(x.dtype)
```

```python
import functools

import jax
import jax.numpy as jnp
from jax import lax
from jax.experimental import pallas as pl
from jax.experimental.pallas import tpu as pltpu

f32, bf16, i32 = jnp.float32, jnp.bfloat16, jnp.int32

D_MODEL = 1024
DEPTH = 4
N_A_LAYERS = DEPTH // 2

D_INNER = 2048
SSM_HEAD_DIM = 64
SSM_HEADS = D_INNER // SSM_HEAD_DIM
SSM_GROUPS = 4
SSM_STATE = 128
CONV_WIDTH = 4
SSM_CHUNK = 128
BC_DIM = SSM_GROUPS * SSM_STATE
CONV_DIM = D_INNER + 2 * BC_DIM
GROUP_COLS = D_INNER // SSM_GROUPS

ATTN_HEADS = 16
ATTN_HEAD_DIM = 64
ATTN_DIM = ATTN_HEADS * ATTN_HEAD_DIM
ROT_DIM = ATTN_HEAD_DIM // 4
ROPE_THETA = 500000.0
MOBA_BLOCK = 256
MOBA_TOPK = 3

N_EXPERTS = 32
TOP_K = 4
D_EXPERT = D_MODEL
SWIGLU_LIMIT = 7.0
SWIGLU_ALPHA = 1.702

LN_EPS = 1e-5
RMS_EPS = 1e-5
DEEPNORM_ALPHA = (2 * DEPTH) ** 0.25

LANES = 128
SUBLANES = 8
VMEM_LIMIT = 52 * 1024 * 1024

ROW_TILE = 512
MOE_TM = 256
ROUTER_TILE = 512
COMBINE_TILE = 128

NEG_INF = float("-inf")


def _nt(a, b):
    return lax.dot_general(a, b, (((1,), (1,)), ((), ())), preferred_element_type=f32)


def _tn(a, b):
    return lax.dot_general(a, b, (((0,), (0,)), ((), ())), preferred_element_type=f32)


def _dot(a, b):
    return jnp.dot(a, b, preferred_element_type=f32)


def _split2(x):
    hi = x.astype(bf16)
    lo = (x - hi.astype(f32)).astype(bf16)
    return hi, lo


def _split3(x):
    hi = x.astype(bf16)
    r = x - hi.astype(f32)
    mid = r.astype(bf16)
    lo = (r - mid.astype(f32)).astype(bf16)
    return hi, mid, lo


def _sigmoid(x):
    return 1.0 / (1.0 + jnp.exp(-x))


def _layer_norm(v, g, b):
    mu = jnp.mean(v, axis=-1, keepdims=True)
    vc = v - mu
    var = jnp.mean(vc * vc, axis=-1, keepdims=True)
    return vc * lax.rsqrt(var + LN_EPS) * g + b


def _params(*sem):
    return pltpu.CompilerParams(dimension_semantics=sem, vmem_limit_bytes=VMEM_LIMIT)


def _in_proj_body(h_ref, wz_ref, wx_ref, wdt_ref, z_ref, xbc_ref, dt_ref):
    x = h_ref[...].astype(bf16)
    for c in range(D_INNER // 1024):
        cs = slice(c * 1024, (c + 1) * 1024)
        z_ref[:, cs] = _dot(x, wz_ref[:, cs]).astype(bf16)
    for c in range(CONV_DIM // 1024):
        cs = slice(c * 1024, (c + 1) * 1024)
        xbc_ref[:, cs] = _dot(x, wx_ref[:, cs]).astype(bf16)
    dt_ref[...] = _dot(x, wdt_ref[...])


def _in_proj(h, wz, wx, wdt):
    n = h.shape[0]
    tm = ROW_TILE
    const = lambda i: (0, 0)
    return pl.pallas_call(
        _in_proj_body,
        grid=(n // tm,),
        in_specs=[pl.BlockSpec((tm, D_MODEL), lambda i: (i, 0)),
                  pl.BlockSpec((D_MODEL, D_INNER), const),
                  pl.BlockSpec((D_MODEL, CONV_DIM), const),
                  pl.BlockSpec((D_MODEL, LANES), const)],
        out_specs=[pl.BlockSpec((tm, D_INNER), lambda i: (i, 0)),
                   pl.BlockSpec((tm, CONV_DIM), lambda i: (i, 0)),
                   pl.BlockSpec((tm, LANES), lambda i: (i, 0))],
        out_shape=[jax.ShapeDtypeStruct((n, D_INNER), bf16),
                   jax.ShapeDtypeStruct((n, CONV_DIM), bf16),
                   jax.ShapeDtypeStruct((n, LANES), f32)],
        compiler_params=_params("parallel"),
        name="ssm_in_proj",
    )(h, wz, wx, wdt)


_EXT_PAD = SUBLANES


def _ssd_body(z_ref, xbc_ref, dt_ref, cw_ref, cb_ref, dtb_ref, alog_ref, dexp_ref, ng_ref, e_ref, tril_ref,
              y_ref, ext_scr, act_scr, st_scr, y_scr, ex_scr):
    c = pl.program_id(1)
    L = SSM_CHUNK

    @pl.when(c == 0)
    def _():
        ext_scr[0:_EXT_PAD, :] = jnp.zeros((_EXT_PAD, CONV_DIM), f32)
        st_scr[...] = jnp.zeros_like(st_scr)

    ext_scr[_EXT_PAD:_EXT_PAD + L, :] = xbc_ref[...].astype(f32)
    for cc in range(CONV_DIM // 512):
        cs = slice(cc * 512, (cc + 1) * 512)
        conv = cb_ref[:, cs] + cw_ref[CONV_WIDTH - 1:CONV_WIDTH, cs] * ext_scr[_EXT_PAD:_EXT_PAD + L, cs]
        for j in range(1, CONV_WIDTH):
            conv = conv + cw_ref[CONV_WIDTH - 1 - j:CONV_WIDTH - j, cs] * ext_scr[_EXT_PAD - j:_EXT_PAD - j + L, cs]
        act_scr[:, cs] = conv * _sigmoid(conv)
    ext_scr[0:_EXT_PAD, :] = ext_scr[L:L + _EXT_PAD, :]

    dtv = dt_ref[...] + dtb_ref[...]
    dt = jnp.maximum(dtv, 0.0) + jnp.log1p(jnp.exp(-jnp.abs(dtv)))
    a = -jnp.exp(alog_ref[...])
    adt = dt * a
    tril = tril_ref[...]
    a1, a2, a3 = _split3(adt)
    acs = _dot(tril, a1) + _dot(tril, a2) + _dot(tril, a3)
    acs_t = acs.T
    dt_t = dt.T
    a_last = acs[L - 1:L, :]
    dte = jnp.exp(a_last - acs)
    ea = jnp.exp(acs)
    cd = jnp.exp(a_last)
    stack = jnp.concatenate([dt * dte, ea, jnp.broadcast_to(cd, (SUBLANES, LANES))], axis=0)
    s_hi, s_lo = _split2(stack)
    ex_scr[...] = _dot(s_hi, e_ref[...]) + _dot(s_lo, e_ref[...])

    row = lax.broadcasted_iota(i32, (L, L), 0)
    col = lax.broadcasted_iota(i32, (L, L), 1)
    lower = row >= col
    lane = lax.broadcasted_iota(i32, (1, LANES), 1)
    first_head = lane < SSM_HEAD_DIM

    for g in range(SSM_GROUPS):
        gs = slice(g * GROUP_COLS, (g + 1) * GROUP_COLS)
        b_g = act_scr[:, D_INNER + g * SSM_STATE:D_INNER + (g + 1) * SSM_STATE].astype(bf16)
        c_g = act_scr[:, D_INNER + BC_DIM + g * SSM_STATE:D_INNER + BC_DIM + (g + 1) * SSM_STATE].astype(bf16)
        cb = _nt(c_g, b_g)
        st_old = st_scr[:, gs]
        xs_g = act_scr[:, gs]
        y_scr[:, gs] = _dot(c_g, st_old.astype(bf16)) * ex_scr[L:2 * L, gs] + dexp_ref[:, gs] * xs_g
        xw = (xs_g * ex_scr[0:L, gs]).astype(bf16)
        st_scr[:, gs] = st_old * ex_scr[2 * L:2 * L + 1, gs] + _tn(b_g, xw)
        for j in range(GROUP_COLS // LANES):
            h0 = g * (GROUP_COLS // SSM_HEAD_DIM) + 2 * j
            ms = []
            for h in (h0, h0 + 1):
                diff = acs[:, h:h + 1] - acs_t[h:h + 1, :]
                decay = jnp.exp(jnp.where(lower, diff, NEG_INF))
                ms.append((cb * decay * dt_t[h:h + 1, :]).astype(bf16))
            lhs = jnp.concatenate(ms, axis=1)
            ps = slice(g * GROUP_COLS + j * LANES, g * GROUP_COLS + (j + 1) * LANES)
            xp = act_scr[:, ps].astype(bf16)
            zero = jnp.zeros_like(xp)
            rhs = jnp.concatenate([jnp.where(first_head, xp, zero), jnp.where(first_head, zero, xp)], axis=0)
            y_scr[:, ps] += _dot(lhs, rhs)

    for g in range(SSM_GROUPS):
        gs = slice(g * GROUP_COLS, (g + 1) * GROUP_COLS)
        zf = z_ref[:, gs].astype(f32)
        hg = y_scr[:, gs] * (zf * _sigmoid(zf))
        ms = jnp.mean(hg * hg, axis=-1, keepdims=True)
        y_ref[:, gs] = (hg * lax.rsqrt(ms + RMS_EPS) * ng_ref[:, gs]).astype(bf16)


def _ssd(z, xbc, dt, conv_w, conv_b, dt_bias, a_log, d_skip, norm_g, bsz, seq):
    n = bsz * seq
    L = SSM_CHUNK
    nc = seq // L
    pad = LANES - SSM_HEADS
    dtb = jnp.pad(dt_bias.astype(f32), (0, pad)).reshape(1, LANES)
    alog = jnp.pad(a_log.astype(f32), (0, pad)).reshape(1, LANES)
    dexp = jnp.repeat(d_skip.astype(f32), SSM_HEAD_DIM).reshape(1, D_INNER)
    head_of_col = jnp.arange(D_INNER, dtype=i32) // SSM_HEAD_DIM
    expand = (jnp.arange(LANES, dtype=i32)[:, None] == head_of_col[None, :]).astype(bf16)
    tril = jnp.tril(jnp.ones((L, L), f32)).astype(bf16)
    tok = lambda b, c: (b * nc + c, 0)
    const = lambda b, c: (0, 0)
    ex_rows = 2 * L + SUBLANES
    return pl.pallas_call(
        _ssd_body,
        grid=(bsz, nc),
        in_specs=[pl.BlockSpec((L, D_INNER), tok),
                  pl.BlockSpec((L, CONV_DIM), tok),
                  pl.BlockSpec((L, LANES), tok),
                  pl.BlockSpec((CONV_WIDTH, CONV_DIM), const),
                  pl.BlockSpec((1, CONV_DIM), const),
                  pl.BlockSpec((1, LANES), const),
                  pl.BlockSpec((1, LANES), const),
                  pl.BlockSpec((1, D_INNER), const),
                  pl.BlockSpec((1, D_INNER), const),
                  pl.BlockSpec((LANES, D_INNER), const),
                  pl.BlockSpec((L, L), const)],
        out_specs=pl.BlockSpec((L, D_INNER), tok),
        out_shape=jax.ShapeDtypeStruct((n, D_INNER), bf16),
        scratch_shapes=[pltpu.VMEM((L + 2 * _EXT_PAD, CONV_DIM), f32),
                        pltpu.VMEM((L, CONV_DIM), f32),
                        pltpu.VMEM((SSM_STATE, D_INNER), f32),
                        pltpu.VMEM((L, D_INNER), f32),
                        pltpu.VMEM((ex_rows, D_INNER), f32)],
        compiler_params=_params("parallel", "arbitrary"),
        name="ssm_ssd",
    )(z, xbc, dt, conv_w.astype(f32), conv_b.astype(f32).reshape(1, CONV_DIM), dtb, alog, dexp,
      norm_g.astype(f32).reshape(1, D_INNER), expand, tril)


def _mm_res_ln_body(y_ref, w_ref, h_ref, g_ref, b_ref, o_ref):
    mix = _dot(y_ref[...], w_ref[...])
    o_ref[...] = _layer_norm(DEEPNORM_ALPHA * h_ref[...] + mix, g_ref[...], b_ref[...])


def _mm_res_ln(y, w, h, g, b):
    n, k = y.shape
    tm = ROW_TILE
    const = lambda i: (0, 0)
    return pl.pallas_call(
        _mm_res_ln_body,
        grid=(n // tm,),
        in_specs=[pl.BlockSpec((tm, k), lambda i: (i, 0)),
                  pl.BlockSpec((k, D_MODEL), const),
                  pl.BlockSpec((tm, D_MODEL), lambda i: (i, 0)),
                  pl.BlockSpec((1, D_MODEL), const),
                  pl.BlockSpec((1, D_MODEL), const)],
        out_specs=pl.BlockSpec((tm, D_MODEL), lambda i: (i, 0)),
        out_shape=jax.ShapeDtypeStruct((n, D_MODEL), f32),
        compiler_params=_params("parallel"),
        name="out_proj_ln",
    )(y, w, h, g.astype(f32).reshape(1, D_MODEL), b.astype(f32).reshape(1, D_MODEL))


def _rope_tables(seq):
    half = ROT_DIM // 2
    inv_freq = ROPE_THETA ** (-jnp.arange(0, ROT_DIM, 2, dtype=f32) / ROT_DIM)
    ang = jnp.arange(seq, dtype=f32)[:, None] * inv_freq[None, :]
    cos, sin = jnp.cos(ang), jnp.sin(ang)
    j = jnp.arange(LANES) % ATTN_HEAD_DIM
    idx = j % half
    first = (j < half)[None, :]
    second = ((j >= half) & (j < ROT_DIM))[None, :]
    cos_t = jnp.where(first | second, cos[:, idx], 1.0)
    sin_a = jnp.where(first, -sin[:, idx], 0.0)
    sin_b = jnp.where(second, sin[:, idx], 0.0)
    return cos_t, sin_a, sin_b


def _rope_cols(x, cos_t, sin_a, sin_b):
    half = ROT_DIM // 2
    return x * cos_t + pltpu.roll(x, LANES - half, axis=1) * sin_a + pltpu.roll(x, half, axis=1) * sin_b


def _q_proj_body(h_ref, w_ref, cos_ref, sa_ref, sb_ref, q_ref):
    x = h_ref[...].astype(bf16)
    scale = ATTN_HEAD_DIM ** -0.5
    for c in range(ATTN_DIM // LANES):
        cs = slice(c * LANES, (c + 1) * LANES)
        q = _rope_cols(_dot(x, w_ref[:, cs]), cos_ref[...], sa_ref[...], sb_ref[...])
        q_ref[:, cs] = (q * scale).astype(bf16)


def _q_proj(h, w, tables, seq):
    n = h.shape[0]
    tm = ROW_TILE
    per_seq = seq // tm
    const = lambda i: (0, 0)
    pos = lambda i: (i % per_seq, 0)
    return pl.pallas_call(
        _q_proj_body,
        grid=(n // tm,),
        in_specs=[pl.BlockSpec((tm, D_MODEL), lambda i: (i, 0)),
                  pl.BlockSpec((D_MODEL, ATTN_DIM), const),
                  pl.BlockSpec((tm, LANES), pos), pl.BlockSpec((tm, LANES), pos), pl.BlockSpec((tm, LANES), pos)],
        out_specs=pl.BlockSpec((tm, ATTN_DIM), lambda i: (i, 0)),
        out_shape=jax.ShapeDtypeStruct((n, ATTN_DIM), bf16),
        compiler_params=_params("parallel"),
        name="attn_q_proj",
    )(h, w, *tables)


def _kv_proj_body(h_ref, wk_ref, wv_ref, cos_ref, sa_ref, sb_ref, k_ref, v_ref, km_ref):
    x = h_ref[...].astype(bf16)
    for c in range(ATTN_DIM // LANES):
        cs = slice(c * LANES, (c + 1) * LANES)
        k = _rope_cols(_dot(x, wk_ref[:, cs]), cos_ref[...], sa_ref[...], sb_ref[...])
        k_ref[:, cs] = k.astype(bf16)
        km_ref[0, :, cs] = jnp.mean(k, axis=0, keepdims=True)
        v_ref[:, cs] = _dot(x, wv_ref[:, cs]).astype(bf16)


def _kv_proj(h, wk, wv, tables, seq):
    n = h.shape[0]
    tm = MOBA_BLOCK
    per_seq = seq // tm
    const = lambda i: (0, 0)
    pos = lambda i: (i % per_seq, 0)
    return pl.pallas_call(
        _kv_proj_body,
        grid=(n // tm,),
        in_specs=[pl.BlockSpec((tm, D_MODEL), lambda i: (i, 0)),
                  pl.BlockSpec((D_MODEL, ATTN_DIM), const),
                  pl.BlockSpec((D_MODEL, ATTN_DIM), const),
                  pl.BlockSpec((tm, LANES), pos), pl.BlockSpec((tm, LANES), pos), pl.BlockSpec((tm, LANES), pos)],
        out_specs=[pl.BlockSpec((tm, ATTN_DIM), lambda i: (i, 0)),
                   pl.BlockSpec((tm, ATTN_DIM), lambda i: (i, 0)),
                   pl.BlockSpec((1, 1, ATTN_DIM), lambda i: (i, 0, 0))],
        out_shape=[jax.ShapeDtypeStruct((n, ATTN_DIM), bf16),
                   jax.ShapeDtypeStruct((n, ATTN_DIM), bf16),
                   jax.ShapeDtypeStruct((n // tm, 1, ATTN_DIM), f32)],
        compiler_params=_params("parallel"),
        name="attn_kv_proj",
    )(h, wk, wv, *tables)


def _moba_body(q_ref, k_ref, v_ref, km_ref, o_ref, *, nb):
    qb = pl.program_id(2)
    blk = MOBA_BLOCK
    q = q_ref[...]
    lane = lax.broadcasted_iota(i32, (1, LANES), 1)
    first_head = lane < ATTN_HEAD_DIM
    zero = jnp.zeros_like(q)
    q2 = jnp.concatenate([jnp.where(first_head, q, zero), jnp.where(first_head, zero, q)], axis=0)
    row = lax.broadcasted_iota(i32, (2 * blk, blk), 0)
    col = lax.broadcasted_iota(i32, (2 * blk, blk), 1)
    causal = (col <= row) & ((row < blk) | (col + blk <= row))
    eye = (lax.broadcasted_iota(i32, (blk, blk), 0) == lax.broadcasted_iota(i32, (blk, blk), 1)).astype(bf16)
    blk_id = lax.broadcasted_iota(i32, (nb, 2 * blk), 0)
    km_hi, km_lo = _split2(km_ref[...])
    past = blk_id < qb
    gate = jnp.where(past, _nt(km_hi, q2) + _nt(km_lo, q2), NEG_INF)
    rank = jnp.zeros((nb, 2 * blk), f32)
    for j2 in range(nb):
        gj = gate[j2:j2 + 1, :]
        beats = (gj > gate) | ((gj == gate) & (blk_id > j2))
        rank = rank + beats.astype(f32)
    sel_t = (past & (rank < MOBA_TOPK)).astype(bf16)
    sel = jnp.concatenate([_nt(eye, sel_t[:, 0:blk]), _nt(eye, sel_t[:, blk:2 * blk])], axis=0)

    for c in range(nb):
        @pl.when(qb == c)
        def _(c=c):
            w = (c + 1) * blk
            s = _nt(q2, k_ref[0:w, :])
            parts = [jnp.where(sel[:, j:j + 1] > 0.5, s[:, j * blk:(j + 1) * blk], NEG_INF) for j in range(c)]
            parts.append(jnp.where(causal, s[:, c * blk:w], NEG_INF))
            s = jnp.concatenate(parts, axis=1) if c else parts[0]
            m = jnp.max(s, axis=-1, keepdims=True)
            p = jnp.exp(s - m)
            l = jnp.sum(p, axis=-1, keepdims=True)
            o = _dot(p.astype(bf16), v_ref[0:w, :]) / l
            o_ref[...] = jnp.where(first_head, o[0:blk, :], o[blk:2 * blk, :]).astype(bf16)


def _moba(q, k, v, kmean, bsz, seq):
    n = bsz * seq
    nb = seq // MOBA_BLOCK
    n_pairs = ATTN_DIM // LANES
    return pl.pallas_call(
        functools.partial(_moba_body, nb=nb),
        grid=(bsz, n_pairs, nb),
        in_specs=[pl.BlockSpec((MOBA_BLOCK, LANES), lambda b, hp, qb: (b * nb + qb, hp)),
                  pl.BlockSpec((seq, LANES), lambda b, hp, qb: (b, hp)),
                  pl.BlockSpec((seq, LANES), lambda b, hp, qb: (b, hp)),
                  pl.BlockSpec((None, nb, LANES), lambda b, hp, qb: (b, 0, hp))],
        out_specs=pl.BlockSpec((MOBA_BLOCK, LANES), lambda b, hp, qb: (b * nb + qb, hp)),
        out_shape=jax.ShapeDtypeStruct((n, ATTN_DIM), bf16),
        compiler_params=_params("parallel", "parallel", "arbitrary"),
        name="moba_attention",
    )(q, k, v, kmean)


def _router_body(h_ref, wt_ref, b_ref, u_ref, idx_ref, gate_ref, rank_ref, cnt_ref, cnt_scr):
    i = pl.program_id(0)

    @pl.when(i == 0)
    def _():
        cnt_scr[...] = jnp.zeros_like(cnt_scr)

    tr = h_ref.shape[0]
    x_hi, x_lo = _split2(h_ref[...])
    w_hi, w_lo = _split2(wt_ref[...])
    logits = _nt(w_hi, x_hi) + _nt(w_hi, x_lo) + _nt(w_lo, x_hi) + b_ref[...]
    e_id = lax.broadcasted_iota(i32, (N_EXPERTS, tr), 0)
    cur = logits
    vals, idxs = [], []
    for _ in range(TOP_K):
        m = jnp.max(cur, axis=0, keepdims=True)
        idx = jnp.min(jnp.where(cur == m, e_id, N_EXPERTS), axis=0, keepdims=True)
        vals.append(m)
        idxs.append(idx)
        cur = jnp.where(e_id == idx, NEG_INF, cur)
    exps = [jnp.exp(v - vals[0]) for v in vals]
    denom = exps[0]
    for e in exps[1:]:
        denom = denom + e
    onehot = jnp.zeros((N_EXPERTS, tr), f32)
    for idx in idxs:
        onehot = onehot + (e_id == idx).astype(f32)
    base = cnt_scr[:, 0:1]
    before = _dot(onehot.astype(bf16), u_ref[...]) + base
    gate_ref[...] = jnp.zeros_like(gate_ref)
    for r in range(TOP_K):
        idx_ref[r:r + 1, :] = idxs[r]
        gate_ref[r:r + 1, :] = exps[r] / denom
        rank_ref[r:r + 1, :] = jnp.sum(jnp.where(e_id == idxs[r], before, 0.0), axis=0, keepdims=True).astype(i32)
    total = base + jnp.sum(onehot, axis=1, keepdims=True)
    cnt_scr[...] = jnp.broadcast_to(total, cnt_scr.shape)
    cnt_ref[...] = jnp.broadcast_to(total, cnt_ref.shape).astype(i32)


def _router(h, w_router, b_router):
    n = h.shape[0]
    tr = ROUTER_TILE
    wt = w_router.astype(f32).T
    b = b_router.astype(f32).reshape(N_EXPERTS, 1)
    strict_upper = (jnp.arange(tr)[:, None] < jnp.arange(tr)[None, :]).astype(bf16)
    const = lambda i: (0, 0)
    tok = lambda i: (0, i)
    return pl.pallas_call(
        _router_body,
        grid=(n // tr,),
        in_specs=[pl.BlockSpec((tr, D_MODEL), lambda i: (i, 0)),
                  pl.BlockSpec((N_EXPERTS, D_MODEL), const),
                  pl.BlockSpec((N_EXPERTS, 1), const),
                  pl.BlockSpec((tr, tr), const)],
        out_specs=[pl.BlockSpec((TOP_K, tr), tok),
                   pl.BlockSpec((SUBLANES, tr), tok),
                   pl.BlockSpec((TOP_K, tr), tok),
                   pl.BlockSpec((N_EXPERTS, LANES), const)],
        out_shape=[jax.ShapeDtypeStruct((TOP_K, n), i32),
                   jax.ShapeDtypeStruct((SUBLANES, n), f32),
                   jax.ShapeDtypeStruct((TOP_K, n), i32),
                   jax.ShapeDtypeStruct((N_EXPERTS, LANES), i32)],
        scratch_shapes=[pltpu.VMEM((N_EXPERTS, LANES), f32)],
        compiler_params=_params("arbitrary"),
        name="moe_router",
    )(h, wt, b, strict_upper)


def _gather_rows(idx_smem, src_hbm, dst_vmem, sem, n_rows, idx_base=0, both_queues=False):
    for r in range(n_rows):
        tok = idx_smem[idx_base + r]
        copy = pltpu.make_async_copy(src_hbm.at[pl.ds(tok, 1)], dst_vmem.at[pl.ds(r, 1)], sem)
        copy.start(priority=r % 2 if both_queues else 0)


def _wait_rows(src_hbm, dst_vmem, sem, n_rows):
    pltpu.make_async_copy(src_hbm.at[pl.ds(0, n_rows)], dst_vmem, sem).wait()


def _idx_copy(idx_hbm, idx_smem, idx_sem, tile, slot, n_idx):
    src = idx_hbm.at[pl.ds(pl.multiple_of(tile * n_idx, n_idx), n_idx)]
    dst = idx_smem.at[pl.ds(pl.multiple_of(slot * n_idx, n_idx), n_idx)]
    return pltpu.make_async_copy(src, dst, idx_sem.at[slot])


def _expert_body(blk_e_ref, nact_ref, tok_hbm, h_hbm, wgu_ref, bgu_ref, wd_ref, bd_ref, y_ref,
                 idx_smem, x_buf, idx_sem, row_sem):
    i = pl.program_id(0)
    tm = x_buf.shape[1]
    nact = nact_ref[0]
    last_blk = pl.num_programs(0) - 1
    slot = i % 2
    nxt = 1 - slot

    @pl.when(i == 0)
    def _():
        first = _idx_copy(tok_hbm, idx_smem, idx_sem, 0, 0, tm)
        first.start()
        first.wait()
        _gather_rows(idx_smem, h_hbm, x_buf.at[0], row_sem.at[0], tm)
        _idx_copy(tok_hbm, idx_smem, idx_sem, jnp.minimum(1, last_blk), 1, tm).start()

    @pl.when(i < nact)
    def _():
        _idx_copy(tok_hbm, idx_smem, idx_sem, jnp.minimum(i + 1, last_blk), nxt, tm).wait()
        _wait_rows(h_hbm, x_buf.at[slot], row_sem.at[slot], tm)
        _gather_rows(idx_smem, h_hbm, x_buf.at[nxt], row_sem.at[nxt], tm, idx_base=nxt * tm)
        x = x_buf[slot].astype(bf16)
        gu = _dot(x, wgu_ref[...]) + bgu_ref[...]
        g_lin = jnp.minimum(gu[:, :D_EXPERT], SWIGLU_LIMIT)
        u_lin = jnp.clip(gu[:, D_EXPERT:], -SWIGLU_LIMIT, SWIGLU_LIMIT)
        act = (u_lin + 1.0) * (g_lin * _sigmoid(g_lin * SWIGLU_ALPHA))
        y_ref[...] = _dot(act.astype(bf16), wd_ref[...]) + bd_ref[...]

        @pl.when(i + 1 < nact)
        def _():
            _idx_copy(tok_hbm, idx_smem, idx_sem, jnp.minimum(i + 2, last_blk), slot, tm).start()

        @pl.when(i + 1 == nact)
        def _():
            _wait_rows(h_hbm, x_buf.at[nxt], row_sem.at[nxt], tm)

    @pl.when(i >= nact)
    def _():
        y_ref[...] = jnp.zeros_like(y_ref)


def _expert_ffn(h, row_tok, blk_e, nact, w_gu, b_gu, w_down, b_down, n_blocks):
    tm = MOE_TM
    e_map = lambda i, be, na: (be[i], 0, 0)
    out_map = lambda i, be, na: (i, 0)
    return pl.pallas_call(
        _expert_body,
        grid_spec=pltpu.PrefetchScalarGridSpec(
            num_scalar_prefetch=2,
            grid=(n_blocks,),
            in_specs=[pl.BlockSpec(memory_space=pl.ANY),
                      pl.BlockSpec(memory_space=pl.ANY),
                      pl.BlockSpec((None, D_MODEL, 2 * D_EXPERT), e_map),
                      pl.BlockSpec((None, 1, 2 * D_EXPERT), e_map),
                      pl.BlockSpec((None, D_EXPERT, D_MODEL), e_map),
                      pl.BlockSpec((None, 1, D_MODEL), e_map)],
            out_specs=pl.BlockSpec((tm, D_MODEL), out_map),
            scratch_shapes=[pltpu.SMEM((2 * tm,), i32),
                            pltpu.VMEM((2, tm, D_MODEL), f32),
                            pltpu.SemaphoreType.DMA((2,)),
                            pltpu.SemaphoreType.DMA((2,))]),
        out_shape=jax.ShapeDtypeStruct((n_blocks * tm, D_MODEL), f32),
        compiler_params=_params("arbitrary"),
        name="moe_experts",
    )(blk_e, nact, row_tok, h, w_gu, b_gu, w_down, b_down)


def _combine_body(dest_hbm, ybuf_hbm, gate_ref, h_ref, g_ref, b_ref, o_ref, idx_smem, rows, idx_sem, row_sem):
    i = pl.program_id(0)
    tc = h_ref.shape[0]
    n_idx = TOP_K * tc
    last = pl.num_programs(0) - 1
    slot = i % 2
    nxt = 1 - slot

    def gather(s):
        for k in range(TOP_K):
            _gather_rows(idx_smem, ybuf_hbm, rows.at[s, k], row_sem.at[s], tc, idx_base=s * n_idx + k * tc,
                         both_queues=True)

    def wait(s):
        for k in range(TOP_K):
            _wait_rows(ybuf_hbm, rows.at[s, k], row_sem.at[s], tc)

    @pl.when(i == 0)
    def _():
        first = _idx_copy(dest_hbm, idx_smem, idx_sem, 0, 0, n_idx)
        first.start()
        first.wait()
        gather(0)
        _idx_copy(dest_hbm, idx_smem, idx_sem, jnp.minimum(1, last), 1, n_idx).start()

    _idx_copy(dest_hbm, idx_smem, idx_sem, jnp.minimum(i + 1, last), nxt, n_idx).wait()
    wait(slot)
    gather(nxt)
    gates = jnp.concatenate([gate_ref[...], jnp.zeros((tc - SUBLANES, tc), f32)], axis=0).T
    moe = rows[slot, 0] * gates[:, 0:1]
    for k in range(1, TOP_K):
        moe = moe + rows[slot, k] * gates[:, k:k + 1]
    o_ref[...] = _layer_norm(DEEPNORM_ALPHA * h_ref[...] + moe, g_ref[...], b_ref[...])

    @pl.when(i < last)
    def _():
        _idx_copy(dest_hbm, idx_smem, idx_sem, jnp.minimum(i + 2, last), slot, n_idx).start()

    @pl.when(i == last)
    def _():
        wait(nxt)


def _combine_ln(dest_tiles, ybuf, gates, h, g, b):
    n = h.shape[0]
    tc = COMBINE_TILE
    const = lambda i: (0, 0)
    return pl.pallas_call(
        _combine_body,
        grid=(n // tc,),
        in_specs=[pl.BlockSpec(memory_space=pl.ANY),
                  pl.BlockSpec(memory_space=pl.ANY),
                  pl.BlockSpec((SUBLANES, tc), lambda i: (0, i)),
                  pl.BlockSpec((tc, D_MODEL), lambda i: (i, 0)),
                  pl.BlockSpec((1, D_MODEL), const),
                  pl.BlockSpec((1, D_MODEL), const)],
        out_specs=pl.BlockSpec((tc, D_MODEL), lambda i: (i, 0)),
        out_shape=jax.ShapeDtypeStruct((n, D_MODEL), f32),
        scratch_shapes=[pltpu.SMEM((2 * TOP_K * tc,), i32),
                        pltpu.VMEM((2, TOP_K, tc, D_MODEL), f32),
                        pltpu.SemaphoreType.DMA((2,)),
                        pltpu.SemaphoreType.DMA((2,))],
        compiler_params=_params("arbitrary"),
        name="moe_combine_ln",
    )(dest_tiles, ybuf, gates, h, g.astype(f32).reshape(1, D_MODEL), b.astype(f32).reshape(1, D_MODEL))


def _moe_layer(h, w_router, b_router, w_gu, b_gu, w_down, b_down, ln_g, ln_b):
    n = h.shape[0]
    tm = MOE_TM
    n_blocks = (n * TOP_K) // tm + N_EXPERTS
    idx, gates, rank, counts = _router(h, w_router, b_router)
    counts = counts[:, 0]
    blocks_per_e = (counts + tm - 1) // tm
    blk_end = jnp.cumsum(blocks_per_e)
    blk_start = blk_end - blocks_per_e
    nact = blk_end[-1:].astype(i32)
    e_ids = jnp.arange(N_EXPERTS, dtype=i32)
    start_of = jnp.sum(jnp.where(idx[:, :, None] == e_ids, blk_start, 0), axis=-1)
    dest = start_of * tm + rank
    blk_ids = jnp.minimum(jnp.arange(n_blocks, dtype=i32), nact[0] - 1)
    blk_e = jnp.minimum(jnp.sum(blk_end[None, :] <= blk_ids[:, None], axis=-1), N_EXPERTS - 1).astype(i32)
    tok_ids = jnp.broadcast_to(jnp.arange(n, dtype=i32)[None, :], (TOP_K, n))
    row_tok = jnp.zeros((n_blocks * tm,), i32).at[dest.reshape(-1)].set(tok_ids.reshape(-1), unique_indices=True)
    ybuf = _expert_ffn(h, row_tok, blk_e, nact, w_gu, b_gu.astype(f32).reshape(N_EXPERTS, 1, 2 * D_EXPERT),
                       w_down, b_down.astype(f32).reshape(N_EXPERTS, 1, D_MODEL), n_blocks)
    tc = COMBINE_TILE
    dest_tiles = dest.reshape(TOP_K, n // tc, tc).transpose(1, 0, 2).reshape(-1)
    return _combine_ln(dest_tiles, ybuf, gates, h, ln_g, ln_b)


def _mamba_layer(h, bsz, seq, w_in, conv_w, conv_b, dt_bias, a_log, d_skip, norm_g, w_out, ln_g, ln_b):
    wz = w_in[:, :D_INNER].astype(bf16)
    wx = w_in[:, D_INNER:D_INNER + CONV_DIM].astype(bf16)
    wdt = jnp.pad(w_in[:, D_INNER + CONV_DIM:], ((0, 0), (0, LANES - SSM_HEADS))).astype(bf16)
    z, xbc, dt = _in_proj(h, wz, wx, wdt)
    y = _ssd(z, xbc, dt, conv_w, conv_b, dt_bias, a_log, d_skip, norm_g, bsz, seq)
    return _mm_res_ln(y, w_out.astype(bf16), h, ln_g, ln_b)


def _moba_layer(h, bsz, seq, w_q, w_o, k, v, kmean, tables, ln_g, ln_b):
    q = _q_proj(h, w_q.astype(bf16), tables, seq)
    o = _moba(q, k, v, kmean, bsz, seq)
    return _mm_res_ln(o, w_o.astype(bf16), h, ln_g, ln_b)


def kernel(x, ssm_w_in, ssm_conv_w, ssm_conv_b, ssm_dt_bias, ssm_a_log, ssm_d, ssm_norm_g, ssm_w_out,
           kv_w_k, kv_w_v, attn_w_q, attn_w_o, moe_w_router, moe_b_router, moe_w_gate_up, moe_b_gate_up,
           moe_w_down, moe_b_down, ln_mix_g, ln_mix_b, ln_ffn_g, ln_ffn_b):
    bsz, seq, _ = x.shape
    h = x.reshape(bsz * seq, D_MODEL).astype(f32)
    tables = _rope_tables(seq)
    k = v = kmean = None
    for l in range(DEPTH):
        if l < N_A_LAYERS:
            h = _mamba_layer(h, bsz, seq, ssm_w_in[l], ssm_conv_w[l], ssm_conv_b[l], ssm_dt_bias[l], ssm_a_log[l],
                             ssm_d[l], ssm_norm_g[l], ssm_w_out[l], ln_mix_g[l], ln_mix_b[l])
        else:
            if l == N_A_LAYERS:
                k, v, kmean = _kv_proj(h, kv_w_k.astype(bf16), kv_w_v.astype(bf16), tables, seq)
                kmean = kmean.reshape(bsz, seq // MOBA_BLOCK, ATTN_DIM)
            j = l - N_A_LAYERS
            h = _moba_layer(h, bsz, seq, attn_w_q[j], attn_w_o[j], k, v, kmean, tables, ln_mix_g[l], ln_mix_b[l])
        h = _moe_layer(h, moe_w_router[l], moe_b_router[l], moe_w_gate_up[l].astype(bf16), moe_b_gate_up[l],
                       moe_w_down[l].astype(bf16), moe_b_down[l], ln_ffn_g[l], ln_ffn_b[l])
    return h.reshape(bsz, seq, D_MODEL).astype(x.dtype)
```

```python
import functools

import jax
import jax.numpy as jnp
from jax import lax
from jax.experimental import pallas as pl
from jax.experimental.pallas import tpu as pltpu

f32, bf16, i32 = jnp.float32, jnp.bfloat16, jnp.int32

D_MODEL = 1024
DEPTH = 4
N_A_LAYERS = DEPTH // 2

D_INNER = 2048
SSM_HEAD_DIM = 64
SSM_HEADS = D_INNER // SSM_HEAD_DIM
SSM_GROUPS = 4
SSM_STATE = 128
CONV_WIDTH = 4
SSM_CHUNK = 128
BC_DIM = SSM_GROUPS * SSM_STATE
CONV_DIM = D_INNER + 2 * BC_DIM
GROUP_COLS = D_INNER // SSM_GROUPS

ATTN_HEADS = 16
ATTN_HEAD_DIM = 64
ATTN_DIM = ATTN_HEADS * ATTN_HEAD_DIM
ROT_DIM = ATTN_HEAD_DIM // 4
ROPE_THETA = 500000.0
MOBA_BLOCK = 256
MOBA_TOPK = 3

N_EXPERTS = 32
TOP_K = 4
D_EXPERT = D_MODEL
SWIGLU_LIMIT = 7.0
SWIGLU_ALPHA = 1.702

LN_EPS = 1e-5
RMS_EPS = 1e-5
DEEPNORM_ALPHA = (2 * DEPTH) ** 0.25

LANES = 128
SUBLANES = 8
VMEM_LIMIT = 52 * 1024 * 1024

ROW_TILE = 512
MOE_TM = 256
ROUTER_TILE = 512
COMBINE_TILE = 128

NEG_INF = float("-inf")


def _nt(a, b):
    return lax.dot_general(a, b, (((1,), (1,)), ((), ())), preferred_element_type=f32)


def _tn(a, b):
    return lax.dot_general(a, b, (((0,), (0,)), ((), ())), preferred_element_type=f32)


def _dot(a, b):
    return jnp.dot(a, b, preferred_element_type=f32)


def _split2(x):
    hi = x.astype(bf16)
    lo = (x - hi.astype(f32)).astype(bf16)
    return hi, lo


def _split3(x):
    hi = x.astype(bf16)
    r = x - hi.astype(f32)
    mid = r.astype(bf16)
    lo = (r - mid.astype(f32)).astype(bf16)
    return hi, mid, lo


def _sigmoid(x):
    return 1.0 / (1.0 + jnp.exp(-x))


def _layer_norm(v, g, b):
    mu = jnp.mean(v, axis=-1, keepdims=True)
    vc = v - mu
    var = jnp.mean(vc * vc, axis=-1, keepdims=True)
    return vc * lax.rsqrt(var + LN_EPS) * g + b


def _params(*sem):
    return pltpu.CompilerParams(dimension_semantics=sem, vmem_limit_bytes=VMEM_LIMIT)


def _in_proj_body(h_ref, wz_ref, wx_ref, wdt_ref, z_ref, xbc_ref, dt_ref):
    x = h_ref[...].astype(bf16)
    for c in range(D_INNER // 1024):
        cs = slice(c * 1024, (c + 1) * 1024)
        z_ref[:, cs] = _dot(x, wz_ref[:, cs]).astype(bf16)
    for c in range(CONV_DIM // 1024):
        cs = slice(c * 1024, (c + 1) * 1024)
        xbc_ref[:, cs] = _dot(x, wx_ref[:, cs]).astype(bf16)
    dt_ref[...] = _dot(x, wdt_ref[...])


def _in_proj(h, wz, wx, wdt):
    n = h.shape[0]
    tm = ROW_TILE
    const = lambda i: (0, 0)
    return pl.pallas_call(
        _in_proj_body,
        grid=(n // tm,),
        in_specs=[pl.BlockSpec((tm, D_MODEL), lambda i: (i, 0)),
                  pl.BlockSpec((D_MODEL, D_INNER), const),
                  pl.BlockSpec((D_MODEL, CONV_DIM), const),
                  pl.BlockSpec((D_MODEL, LANES), const)],
        out_specs=[pl.BlockSpec((tm, D_INNER), lambda i: (i, 0)),
                   pl.BlockSpec((tm, CONV_DIM), lambda i: (i, 0)),
                   pl.BlockSpec((tm, LANES), lambda i: (i, 0))],
        out_shape=[jax.ShapeDtypeStruct((n, D_INNER), bf16),
                   jax.ShapeDtypeStruct((n, CONV_DIM), bf16),
                   jax.ShapeDtypeStruct((n, LANES), f32)],
        compiler_params=_params("parallel"),
        name="ssm_in_proj",
    )(h, wz, wx, wdt)


_EXT_PAD = SUBLANES


def _ssd_body(z_ref, xbc_ref, dt_ref, cw_ref, cb_ref, dtb_ref, alog_ref, dexp_ref, ng_ref, e_ref, tril_ref,
              y_ref, ext_scr, act_scr, st_scr, y_scr, ex_scr):
    c = pl.program_id(1)
    L = SSM_CHUNK

    @pl.when(c == 0)
    def _():
        ext_scr[0:_EXT_PAD, :] = jnp.zeros((_EXT_PAD, CONV_DIM), f32)
        st_scr[...] = jnp.zeros_like(st_scr)

    ext_scr[_EXT_PAD:_EXT_PAD + L, :] = xbc_ref[...].astype(f32)
    for cc in range(CONV_DIM // 512):
        cs = slice(cc * 512, (cc + 1) * 512)
        conv = cb_ref[:, cs] + cw_ref[CONV_WIDTH - 1:CONV_WIDTH, cs] * ext_scr[_EXT_PAD:_EXT_PAD + L, cs]
        for j in range(1, CONV_WIDTH):
            conv = conv + cw_ref[CONV_WIDTH - 1 - j:CONV_WIDTH - j, cs] * ext_scr[_EXT_PAD - j:_EXT_PAD - j + L, cs]
        act_scr[:, cs] = conv * _sigmoid(conv)
    ext_scr[0:_EXT_PAD, :] = ext_scr[L:L + _EXT_PAD, :]

    dtv = dt_ref[...] + dtb_ref[...]
    dt = jnp.maximum(dtv, 0.0) + jnp.log1p(jnp.exp(-jnp.abs(dtv)))
    a = -jnp.exp(alog_ref[...])
    adt = dt * a
    tril = tril_ref[...]
    a1, a2, a3 = _split3(adt)
    acs = _dot(tril, a1) + _dot(tril, a2) + _dot(tril, a3)
    acs_t = acs.T
    dt_t = dt.T
    a_last = acs[L - 1:L, :]
    dte = jnp.exp(a_last - acs)
    ea = jnp.exp(acs)
    cd = jnp.exp(a_last)
    stack = jnp.concatenate([dt * dte, ea, jnp.broadcast_to(cd, (SUBLANES, LANES))], axis=0)
    s_hi, s_lo = _split2(stack)
    ex_scr[...] = _dot(s_hi, e_ref[...]) + _dot(s_lo, e_ref[...])

    row = lax.broadcasted_iota(i32, (L, L), 0)
    col = lax.broadcasted_iota(i32, (L, L), 1)
    lower = row >= col
    lane = lax.broadcasted_iota(i32, (1, LANES), 1)
    first_head = lane < SSM_HEAD_DIM

    for g in range(SSM_GROUPS):
        gs = slice(g * GROUP_COLS, (g + 1) * GROUP_COLS)
        b_g = act_scr[:, D_INNER + g * SSM_STATE:D_INNER + (g + 1) * SSM_STATE].astype(bf16)
        c_g = act_scr[:, D_INNER + BC_DIM + g * SSM_STATE:D_INNER + BC_DIM + (g + 1) * SSM_STATE].astype(bf16)
        cb = _nt(c_g, b_g)
        st_old = st_scr[:, gs]
        xs_g = act_scr[:, gs]
        y_scr[:, gs] = _dot(c_g, st_old.astype(bf16)) * ex_scr[L:2 * L, gs] + dexp_ref[:, gs] * xs_g
        xw = (xs_g * ex_scr[0:L, gs]).astype(bf16)
        st_scr[:, gs] = st_old * ex_scr[2 * L:2 * L + 1, gs] + _tn(b_g, xw)
        for j in range(GROUP_COLS // LANES):
            h0 = g * (GROUP_COLS // SSM_HEAD_DIM) + 2 * j
            ms = []
            for h in (h0, h0 + 1):
                diff = acs[:, h:h + 1] - acs_t[h:h + 1, :]
                decay = jnp.exp(jnp.where(lower, diff, NEG_INF))
                ms.append((cb * decay * dt_t[h:h + 1, :]).astype(bf16))
            lhs = jnp.concatenate(ms, axis=1)
            ps = slice(g * GROUP_COLS + j * LANES, g * GROUP_COLS + (j + 1) * LANES)
            xp = act_scr[:, ps].astype(bf16)
            zero = jnp.zeros_like(xp)
            rhs = jnp.concatenate([jnp.where(first_head, xp, zero), jnp.where(first_head, zero, xp)], axis=0)
            y_scr[:, ps] += _dot(lhs, rhs)

    for g in range(SSM_GROUPS):
        gs = slice(g * GROUP_COLS, (g + 1) * GROUP_COLS)
        zf = z_ref[:, gs].astype(f32)
        hg = y_scr[:, gs] * (zf * _sigmoid(zf))
        ms = jnp.mean(hg * hg, axis=-1, keepdims=True)
        y_ref[:, gs] = (hg * lax.rsqrt(ms + RMS_EPS) * ng_ref[:, gs]).astype(bf16)


def _ssd(z, xbc, dt, conv_w, conv_b, dt_bias, a_log, d_skip, norm_g, bsz, seq):
    n = bsz * seq
    L = SSM_CHUNK
    nc = seq // L
    pad = LANES - SSM_HEADS
    dtb = jnp.pad(dt_bias.astype(f32), (0, pad)).reshape(1, LANES)
    alog = jnp.pad(a_log.astype(f32), (0, pad)).reshape(1, LANES)
    dexp = jnp.repeat(d_skip.astype(f32), SSM_HEAD_DIM).reshape(1, D_INNER)
    head_of_col = jnp.arange(D_INNER, dtype=i32) // SSM_HEAD_DIM
    expand = (jnp.arange(LANES, dtype=i32)[:, None] == head_of_col[None, :]).astype(bf16)
    tril = jnp.tril(jnp.ones((L, L), f32)).astype(bf16)
    tok = lambda b, c: (b * nc + c, 0)
    const = lambda b, c: (0, 0)
    ex_rows = 2 * L + SUBLANES
    return pl.pallas_call(
        _ssd_body,
        grid=(bsz, nc),
        in_specs=[pl.BlockSpec((L, D_INNER), tok),
                  pl.BlockSpec((L, CONV_DIM), tok),
                  pl.BlockSpec((L, LANES), tok),
                  pl.BlockSpec((CONV_WIDTH, CONV_DIM), const),
                  pl.BlockSpec((1, CONV_DIM), const),
                  pl.BlockSpec((1, LANES), const),
                  pl.BlockSpec((1, LANES), const),
                  pl.BlockSpec((1, D_INNER), const),
                  pl.BlockSpec((1, D_INNER), const),
                  pl.BlockSpec((LANES, D_INNER), const),
                  pl.BlockSpec((L, L), const)],
        out_specs=pl.BlockSpec((L, D_INNER), tok),
        out_shape=jax.ShapeDtypeStruct((n, D_INNER), bf16),
        scratch_shapes=[pltpu.VMEM((L + 2 * _EXT_PAD, CONV_DIM), f32),
                        pltpu.VMEM((L, CONV_DIM), f32),
                        pltpu.VMEM((SSM_STATE, D_INNER), f32),
                        pltpu.VMEM((L, D_INNER), f32),
                        pltpu.VMEM((ex_rows, D_INNER), f32)],
        compiler_params=_params("parallel", "arbitrary"),
        name="ssm_ssd",
    )(z, xbc, dt, conv_w.astype(f32), conv_b.astype(f32).reshape(1, CONV_DIM), dtb, alog, dexp,
      norm_g.astype(f32).reshape(1, D_INNER), expand, tril)


def _mm_res_ln_body(y_ref, w_ref, h_ref, g_ref, b_ref, o_ref):
    mix = _dot(y_ref[...], w_ref[...])
    o_ref[...] = _layer_norm(DEEPNORM_ALPHA * h_ref[...] + mix, g_ref[...], b_ref[...])


def _mm_res_ln(y, w, h, g, b):
    n, k = y.shape
    tm = ROW_TILE
    const = lambda i: (0, 0)
    return pl.pallas_call(
        _mm_res_ln_body,
        grid=(n // tm,),
        in_specs=[pl.BlockSpec((tm, k), lambda i: (i, 0)),
                  pl.BlockSpec((k, D_MODEL), const),
                  pl.BlockSpec((tm, D_MODEL), lambda i: (i, 0)),
                  pl.BlockSpec((1, D_MODEL), const),
                  pl.BlockSpec((1, D_MODEL), const)],
        out_specs=pl.BlockSpec((tm, D_MODEL), lambda i: (i, 0)),
        out_shape=jax.ShapeDtypeStruct((n, D_MODEL), f32),
        compiler_params=_params("parallel"),
        name="out_proj_ln",
    )(y, w, h, g.astype(f32).reshape(1, D_MODEL), b.astype(f32).reshape(1, D_MODEL))


def _rope_tables(seq):
    half = ROT_DIM // 2
    inv_freq = ROPE_THETA ** (-jnp.arange(0, ROT_DIM, 2, dtype=f32) / ROT_DIM)
    ang = jnp.arange(seq, dtype=f32)[:, None] * inv_freq[None, :]
    cos, sin = jnp.cos(ang), jnp.sin(ang)
    j = jnp.arange(LANES) % ATTN_HEAD_DIM
    idx = j % half
    first = (j < half)[None, :]
    second = ((j >= half) & (j < ROT_DIM))[None, :]
    cos_t = jnp.where(first | second, cos[:, idx], 1.0)
    sin_a = jnp.where(first, -sin[:, idx], 0.0)
    sin_b = jnp.where(second, sin[:, idx], 0.0)
    return cos_t, sin_a, sin_b


def _rope_cols(x, cos_t, sin_a, sin_b):
    half = ROT_DIM // 2
    return x * cos_t + pltpu.roll(x, LANES - half, axis=1) * sin_a + pltpu.roll(x, half, axis=1) * sin_b


def _q_proj_body(h_ref, w_ref, cos_ref, sa_ref, sb_ref, q_ref):
    x = h_ref[...].astype(bf16)
    scale = ATTN_HEAD_DIM ** -0.5
    for c in range(ATTN_DIM // LANES):
        cs = slice(c * LANES, (c + 1) * LANES)
        q = _rope_cols(_dot(x, w_ref[:, cs]), cos_ref[...], sa_ref[...], sb_ref[...])
        q_ref[:, cs] = (q * scale).astype(bf16)


def _q_proj(h, w, tables, seq):
    n = h.shape[0]
    tm = ROW_TILE
    per_seq = seq // tm
    const = lambda i: (0, 0)
    pos = lambda i: (i % per_seq, 0)
    return pl.pallas_call(
        _q_proj_body,
        grid=(n // tm,),
        in_specs=[pl.BlockSpec((tm, D_MODEL), lambda i: (i, 0)),
                  pl.BlockSpec((D_MODEL, ATTN_DIM), const),
                  pl.BlockSpec((tm, LANES), pos), pl.BlockSpec((tm, LANES), pos), pl.BlockSpec((tm, LANES), pos)],
        out_specs=pl.BlockSpec((tm, ATTN_DIM), lambda i: (i, 0)),
        out_shape=jax.ShapeDtypeStruct((n, ATTN_DIM), bf16),
        compiler_params=_params("parallel"),
        name="attn_q_proj",
    )(h, w, *tables)


def _kv_proj_body(h_ref, wk_ref, wv_ref, cos_ref, sa_ref, sb_ref, k_ref, v_ref, km_ref):
    x = h_ref[...].astype(bf16)
    for c in range(ATTN_DIM // LANES):
        cs = slice(c * LANES, (c + 1) * LANES)
        k = _rope_cols(_dot(x, wk_ref[:, cs]), cos_ref[...], sa_ref[...], sb_ref[...])
        k_ref[:, cs] = k.astype(bf16)
        km_ref[0, :, cs] = jnp.mean(k, axis=0, keepdims=True)
        v_ref[:, cs] = _dot(x, wv_ref[:, cs]).astype(bf16)


def _kv_proj(h, wk, wv, tables, seq):
    n = h.shape[0]
    tm = MOBA_BLOCK
    per_seq = seq // tm
    const = lambda i: (0, 0)
    pos = lambda i: (i % per_seq, 0)
    return pl.pallas_call(
        _kv_proj_body,
        grid=(n // tm,),
        in_specs=[pl.BlockSpec((tm, D_MODEL), lambda i: (i, 0)),
                  pl.BlockSpec((D_MODEL, ATTN_DIM), const),
                  pl.BlockSpec((D_MODEL, ATTN_DIM), const),
                  pl.BlockSpec((tm, LANES), pos), pl.BlockSpec((tm, LANES), pos), pl.BlockSpec((tm, LANES), pos)],
        out_specs=[pl.BlockSpec((tm, ATTN_DIM), lambda i: (i, 0)),
                   pl.BlockSpec((tm, ATTN_DIM), lambda i: (i, 0)),
                   pl.BlockSpec((1, 1, ATTN_DIM), lambda i: (i, 0, 0))],
        out_shape=[jax.ShapeDtypeStruct((n, ATTN_DIM), bf16),
                   jax.ShapeDtypeStruct((n, ATTN_DIM), bf16),
                   jax.ShapeDtypeStruct((n // tm, 1, ATTN_DIM), f32)],
        compiler_params=_params("parallel"),
        name="attn_kv_proj",
    )(h, wk, wv, *tables)


def _moba_body(q_ref, k_ref, v_ref, km_ref, o_ref, *, nb):
    qb = pl.program_id(2)
    blk = MOBA_BLOCK
    q = q_ref[...]
    lane = lax.broadcasted_iota(i32, (1, LANES), 1)
    first_head = lane < ATTN_HEAD_DIM
    zero = jnp.zeros_like(q)
    q2 = jnp.concatenate([jnp.where(first_head, q, zero), jnp.where(first_head, zero, q)], axis=0)
    row = lax.broadcasted_iota(i32, (2 * blk, blk), 0)
    col = lax.broadcasted_iota(i32, (2 * blk, blk), 1)
    causal = (col <= row) & ((row < blk) | (col + blk <= row))
    eye = (lax.broadcasted_iota(i32, (blk, blk), 0) == lax.broadcasted_iota(i32, (blk, blk), 1)).astype(bf16)
    blk_id = lax.broadcasted_iota(i32, (nb, 2 * blk), 0)
    km_hi, km_lo = _split2(km_ref[...])
    past = blk_id < qb
    gate = jnp.where(past, _nt(km_hi, q2) + _nt(km_lo, q2), NEG_INF)
    rank = jnp.zeros((nb, 2 * blk), f32)
    for j2 in range(nb):
        gj = gate[j2:j2 + 1, :]
        beats = (gj > gate) | ((gj == gate) & (blk_id > j2))
        rank = rank + beats.astype(f32)
    sel_t = (past & (rank < MOBA_TOPK)).astype(bf16)
    sel = jnp.concatenate([_nt(eye, sel_t[:, 0:blk]), _nt(eye, sel_t[:, blk:2 * blk])], axis=0)

    own = pl.multiple_of(qb * blk, blk)
    s = jnp.where(causal, _nt(q2, k_ref[pl.ds(own, blk), :]), NEG_INF)
    m0 = jnp.max(s, axis=-1, keepdims=True)
    p = jnp.exp(s - m0)
    l0 = jnp.sum(p, axis=-1, keepdims=True)
    acc0 = _dot(p.astype(bf16), v_ref[pl.ds(own, blk), :])

    for c in range(nb):
        @pl.when(qb == c)
        def _(c=c):
            m, l, acc = m0, l0, acc0
            for j in range(c):
                ks = slice(j * blk, (j + 1) * blk)
                s = jnp.where(sel[:, j:j + 1] > 0.5, _nt(q2, k_ref[ks, :]), NEG_INF)
                m_new = jnp.maximum(m, jnp.max(s, axis=-1, keepdims=True))
                alpha = jnp.exp(m - m_new)
                p = jnp.exp(s - m_new)
                l = alpha * l + jnp.sum(p, axis=-1, keepdims=True)
                acc = alpha * acc + _dot(p.astype(bf16), v_ref[ks, :])
                m = m_new
            o = acc / l
            o_ref[...] = jnp.where(first_head, o[0:blk, :], o[blk:2 * blk, :]).astype(bf16)


def _moba(q, k, v, kmean, bsz, seq):
    n = bsz * seq
    nb = seq // MOBA_BLOCK
    n_pairs = ATTN_DIM // LANES
    return pl.pallas_call(
        functools.partial(_moba_body, nb=nb),
        grid=(bsz, n_pairs, nb),
        in_specs=[pl.BlockSpec((MOBA_BLOCK, LANES), lambda b, hp, qb: (b * nb + qb, hp)),
                  pl.BlockSpec((seq, LANES), lambda b, hp, qb: (b, hp)),
                  pl.BlockSpec((seq, LANES), lambda b, hp, qb: (b, hp)),
                  pl.BlockSpec((None, nb, LANES), lambda b, hp, qb: (b, 0, hp))],
        out_specs=pl.BlockSpec((MOBA_BLOCK, LANES), lambda b, hp, qb: (b * nb + qb, hp)),
        out_shape=jax.ShapeDtypeStruct((n, ATTN_DIM), bf16),
        compiler_params=_params("parallel", "parallel", "arbitrary"),
        name="moba_attention",
    )(q, k, v, kmean)


def _router_body(h_ref, wt_ref, b_ref, u_ref, idx_ref, gate_ref, rank_ref, cnt_ref, cnt_scr):
    i = pl.program_id(0)

    @pl.when(i == 0)
    def _():
        cnt_scr[...] = jnp.zeros_like(cnt_scr)

    tr = h_ref.shape[0]
    x_hi, x_lo = _split2(h_ref[...])
    w_hi, w_lo = _split2(wt_ref[...])
    logits = _nt(w_hi, x_hi) + _nt(w_hi, x_lo) + _nt(w_lo, x_hi) + b_ref[...]
    e_id = lax.broadcasted_iota(i32, (N_EXPERTS, tr), 0)
    cur = logits
    vals, idxs = [], []
    for _ in range(TOP_K):
        m = jnp.max(cur, axis=0, keepdims=True)
        idx = jnp.min(jnp.where(cur == m, e_id, N_EXPERTS), axis=0, keepdims=True)
        vals.append(m)
        idxs.append(idx)
        cur = jnp.where(e_id == idx, NEG_INF, cur)
    exps = [jnp.exp(v - vals[0]) for v in vals]
    denom = exps[0]
    for e in exps[1:]:
        denom = denom + e
    onehot = jnp.zeros((N_EXPERTS, tr), f32)
    for idx in idxs:
        onehot = onehot + (e_id == idx).astype(f32)
    base = cnt_scr[:, 0:1]
    before = _dot(onehot.astype(bf16), u_ref[...]) + base
    gate_ref[...] = jnp.zeros_like(gate_ref)
    for r in range(TOP_K):
        idx_ref[r:r + 1, :] = idxs[r]
        gate_ref[r:r + 1, :] = exps[r] / denom
        rank_ref[r:r + 1, :] = jnp.sum(jnp.where(e_id == idxs[r], before, 0.0), axis=0, keepdims=True).astype(i32)
    total = base + jnp.sum(onehot, axis=1, keepdims=True)
    cnt_scr[...] = jnp.broadcast_to(total, cnt_scr.shape)
    cnt_ref[...] = jnp.broadcast_to(total, cnt_ref.shape).astype(i32)


def _router(h, w_router, b_router):
    n = h.shape[0]
    tr = ROUTER_TILE
    wt = w_router.astype(f32).T
    b = b_router.astype(f32).reshape(N_EXPERTS, 1)
    strict_upper = (jnp.arange(tr)[:, None] < jnp.arange(tr)[None, :]).astype(bf16)
    const = lambda i: (0, 0)
    tok = lambda i: (0, i)
    return pl.pallas_call(
        _router_body,
        grid=(n // tr,),
        in_specs=[pl.BlockSpec((tr, D_MODEL), lambda i: (i, 0)),
                  pl.BlockSpec((N_EXPERTS, D_MODEL), const),
                  pl.BlockSpec((N_EXPERTS, 1), const),
                  pl.BlockSpec((tr, tr), const)],
        out_specs=[pl.BlockSpec((TOP_K, tr), tok),
                   pl.BlockSpec((SUBLANES, tr), tok),
                   pl.BlockSpec((TOP_K, tr), tok),
                   pl.BlockSpec((N_EXPERTS, LANES), const)],
        out_shape=[jax.ShapeDtypeStruct((TOP_K, n), i32),
                   jax.ShapeDtypeStruct((SUBLANES, n), f32),
                   jax.ShapeDtypeStruct((TOP_K, n), i32),
                   jax.ShapeDtypeStruct((N_EXPERTS, LANES), i32)],
        scratch_shapes=[pltpu.VMEM((N_EXPERTS, LANES), f32)],
        compiler_params=_params("arbitrary"),
        name="moe_router",
    )(h, wt, b, strict_upper)


def _gather_rows(idx_smem, src_hbm, dst_vmem, sem, n_rows, idx_base=0, both_queues=False):
    for r in range(n_rows):
        tok = idx_smem[idx_base + r]
        copy = pltpu.make_async_copy(src_hbm.at[pl.ds(tok, 1)], dst_vmem.at[pl.ds(r, 1)], sem)
        copy.start(priority=r % 2 if both_queues else 0)


def _wait_rows(src_hbm, dst_vmem, sem, n_rows):
    pltpu.make_async_copy(src_hbm.at[pl.ds(0, n_rows)], dst_vmem, sem).wait()


def _idx_copy(idx_hbm, idx_smem, idx_sem, tile, slot, n_idx):
    src = idx_hbm.at[pl.ds(pl.multiple_of(tile * n_idx, n_idx), n_idx)]
    dst = idx_smem.at[pl.ds(pl.multiple_of(slot * n_idx, n_idx), n_idx)]
    return pltpu.make_async_copy(src, dst, idx_sem.at[slot])


def _expert_body(blk_e_ref, nact_ref, first_ref, tok_hbm, h_hbm, wgu_ref, bgu_ref, wd_ref, bd_ref, y_ref,
                 idx_smem, x_buf, wgu_bf, wd_bf, idx_sem, row_sem):
    i = pl.program_id(0)
    tm = x_buf.shape[1]
    nact = nact_ref[0]
    last_blk = pl.num_programs(0) - 1
    slot = i % 2
    nxt = 1 - slot

    @pl.when(i == 0)
    def _():
        first = _idx_copy(tok_hbm, idx_smem, idx_sem, 0, 0, tm)
        first.start()
        first.wait()
        _gather_rows(idx_smem, h_hbm, x_buf.at[0], row_sem.at[0], tm, both_queues=True)
        _idx_copy(tok_hbm, idx_smem, idx_sem, jnp.minimum(1, last_blk), 1, tm).start()

    @pl.when(i < nact)
    def _():
        _idx_copy(tok_hbm, idx_smem, idx_sem, jnp.minimum(i + 1, last_blk), nxt, tm).wait()
        _wait_rows(h_hbm, x_buf.at[slot], row_sem.at[slot], tm)
        _gather_rows(idx_smem, h_hbm, x_buf.at[nxt], row_sem.at[nxt], tm, idx_base=nxt * tm, both_queues=True)

        @pl.when(first_ref[i] == 1)
        def _():
            for c in range(2 * D_EXPERT // 512):
                cs = slice(c * 512, (c + 1) * 512)
                wgu_bf[:, cs] = wgu_ref[:, cs].astype(bf16)
            for c in range(D_MODEL // 512):
                cs = slice(c * 512, (c + 1) * 512)
                wd_bf[:, cs] = wd_ref[:, cs].astype(bf16)

        x = x_buf[slot].astype(bf16)
        gu = _dot(x, wgu_bf[...]) + bgu_ref[...]
        g_lin = jnp.minimum(gu[:, :D_EXPERT], SWIGLU_LIMIT)
        u_lin = jnp.clip(gu[:, D_EXPERT:], -SWIGLU_LIMIT, SWIGLU_LIMIT)
        act = (u_lin + 1.0) * (g_lin * _sigmoid(g_lin * SWIGLU_ALPHA))
        y_ref[...] = _dot(act.astype(bf16), wd_bf[...]) + bd_ref[...]

        @pl.when(i + 1 < nact)
        def _():
            _idx_copy(tok_hbm, idx_smem, idx_sem, jnp.minimum(i + 2, last_blk), slot, tm).start()

        @pl.when(i + 1 == nact)
        def _():
            _wait_rows(h_hbm, x_buf.at[nxt], row_sem.at[nxt], tm)

    @pl.when(i >= nact)
    def _():
        y_ref[...] = jnp.zeros_like(y_ref)


def _expert_ffn(h, row_tok, blk_e, nact, first, layer, w_gu, b_gu, w_down, b_down, n_blocks):
    tm = MOE_TM
    w_map = lambda i, be, na, fi: (layer, be[i], 0, 0)
    b_map = lambda i, be, na, fi: (be[i], 0, 0)
    out_map = lambda i, be, na, fi: (i, 0)
    return pl.pallas_call(
        _expert_body,
        grid_spec=pltpu.PrefetchScalarGridSpec(
            num_scalar_prefetch=3,
            grid=(n_blocks,),
            in_specs=[pl.BlockSpec(memory_space=pl.ANY),
                      pl.BlockSpec(memory_space=pl.ANY),
                      pl.BlockSpec((None, None, D_MODEL, 2 * D_EXPERT), w_map),
                      pl.BlockSpec((None, 1, 2 * D_EXPERT), b_map),
                      pl.BlockSpec((None, None, D_EXPERT, D_MODEL), w_map),
                      pl.BlockSpec((None, 1, D_MODEL), b_map)],
            out_specs=pl.BlockSpec((tm, D_MODEL), out_map),
            scratch_shapes=[pltpu.SMEM((2 * tm,), i32),
                            pltpu.VMEM((2, tm, D_MODEL), f32),
                            pltpu.VMEM((D_MODEL, 2 * D_EXPERT), bf16),
                            pltpu.VMEM((D_EXPERT, D_MODEL), bf16),
                            pltpu.SemaphoreType.DMA((2,)),
                            pltpu.SemaphoreType.DMA((2,))]),
        out_shape=jax.ShapeDtypeStruct((n_blocks * tm, D_MODEL), f32),
        compiler_params=_params("arbitrary"),
        name="moe_experts",
    )(blk_e, nact, first, row_tok, h, w_gu, b_gu, w_down, b_down)


def _combine_body(dest_hbm, ybuf_hbm, gate_ref, h_ref, g_ref, b_ref, o_ref, idx_smem, rows, idx_sem, row_sem):
    i = pl.program_id(0)
    tc = h_ref.shape[0]
    n_idx = TOP_K * tc
    last = pl.num_programs(0) - 1
    slot = i % 2
    nxt = 1 - slot

    def gather(s):
        for k in range(TOP_K):
            _gather_rows(idx_smem, ybuf_hbm, rows.at[s, k], row_sem.at[s], tc, idx_base=s * n_idx + k * tc,
                         both_queues=True)

    def wait(s):
        for k in range(TOP_K):
            _wait_rows(ybuf_hbm, rows.at[s, k], row_sem.at[s], tc)

    @pl.when(i == 0)
    def _():
        first = _idx_copy(dest_hbm, idx_smem, idx_sem, 0, 0, n_idx)
        first.start()
        first.wait()
        gather(0)
        _idx_copy(dest_hbm, idx_smem, idx_sem, jnp.minimum(1, last), 1, n_idx).start()

    _idx_copy(dest_hbm, idx_smem, idx_sem, jnp.minimum(i + 1, last), nxt, n_idx).wait()
    wait(slot)
    gather(nxt)
    gates = jnp.concatenate([gate_ref[...], jnp.zeros((tc - SUBLANES, tc), f32)], axis=0).T
    moe = rows[slot, 0] * gates[:, 0:1]
    for k in range(1, TOP_K):
        moe = moe + rows[slot, k] * gates[:, k:k + 1]
    o_ref[...] = _layer_norm(DEEPNORM_ALPHA * h_ref[...] + moe, g_ref[...], b_ref[...])

    @pl.when(i < last)
    def _():
        _idx_copy(dest_hbm, idx_smem, idx_sem, jnp.minimum(i + 2, last), slot, n_idx).start()

    @pl.when(i == last)
    def _():
        wait(nxt)


def _combine_ln(dest_tiles, ybuf, gates, h, g, b):
    n = h.shape[0]
    tc = COMBINE_TILE
    const = lambda i: (0, 0)
    return pl.pallas_call(
        _combine_body,
        grid=(n // tc,),
        in_specs=[pl.BlockSpec(memory_space=pl.ANY),
                  pl.BlockSpec(memory_space=pl.ANY),
                  pl.BlockSpec((SUBLANES, tc), lambda i: (0, i)),
                  pl.BlockSpec((tc, D_MODEL), lambda i: (i, 0)),
                  pl.BlockSpec((1, D_MODEL), const),
                  pl.BlockSpec((1, D_MODEL), const)],
        out_specs=pl.BlockSpec((tc, D_MODEL), lambda i: (i, 0)),
        out_shape=jax.ShapeDtypeStruct((n, D_MODEL), f32),
        scratch_shapes=[pltpu.SMEM((2 * TOP_K * tc,), i32),
                        pltpu.VMEM((2, TOP_K, tc, D_MODEL), f32),
                        pltpu.SemaphoreType.DMA((2,)),
                        pltpu.SemaphoreType.DMA((2,))],
        compiler_params=_params("arbitrary"),
        name="moe_combine_ln",
    )(dest_tiles, ybuf, gates, h, g.astype(f32).reshape(1, D_MODEL), b.astype(f32).reshape(1, D_MODEL))


def _moe_layer(h, layer, w_router, b_router, w_gu, b_gu, w_down, b_down, ln_g, ln_b):
    n = h.shape[0]
    tm = MOE_TM
    n_blocks = (n * TOP_K) // tm + N_EXPERTS
    idx, gates, rank, counts = _router(h, w_router, b_router)
    counts = counts[:, 0]
    blocks_per_e = (counts + tm - 1) // tm
    blk_end = jnp.cumsum(blocks_per_e)
    blk_start = blk_end - blocks_per_e
    nact = blk_end[-1:].astype(i32)
    e_ids = jnp.arange(N_EXPERTS, dtype=i32)
    start_of = jnp.sum(jnp.where(idx[:, :, None] == e_ids, blk_start, 0), axis=-1)
    dest = start_of * tm + rank
    blk_ids = jnp.minimum(jnp.arange(n_blocks, dtype=i32), nact[0] - 1)
    blk_e = jnp.minimum(jnp.sum(blk_end[None, :] <= blk_ids[:, None], axis=-1), N_EXPERTS - 1).astype(i32)
    tok_ids = jnp.broadcast_to(jnp.arange(n, dtype=i32)[None, :], (TOP_K, n))
    row_tok = jnp.zeros((n_blocks * tm,), i32).at[dest.reshape(-1)].set(tok_ids.reshape(-1), unique_indices=True)
    first = jnp.concatenate([jnp.ones((1,), i32), (blk_e[1:] != blk_e[:-1]).astype(i32)])
    ybuf = _expert_ffn(h, row_tok, blk_e, nact, first, layer, w_gu,
                       b_gu.astype(f32).reshape(N_EXPERTS, 1, 2 * D_EXPERT),
                       w_down, b_down.astype(f32).reshape(N_EXPERTS, 1, D_MODEL), n_blocks)
    tc = COMBINE_TILE
    dest_tiles = dest.reshape(TOP_K, n // tc, tc).transpose(1, 0, 2).reshape(-1)
    return _combine_ln(dest_tiles, ybuf, gates, h, ln_g, ln_b)


def _mamba_layer(h, bsz, seq, w_in, conv_w, conv_b, dt_bias, a_log, d_skip, norm_g, w_out, ln_g, ln_b):
    wz = w_in[:, :D_INNER].astype(bf16)
    wx = w_in[:, D_INNER:D_INNER + CONV_DIM].astype(bf16)
    wdt = jnp.pad(w_in[:, D_INNER + CONV_DIM:], ((0, 0), (0, LANES - SSM_HEADS))).astype(bf16)
    z, xbc, dt = _in_proj(h, wz, wx, wdt)
    y = _ssd(z, xbc, dt, conv_w, conv_b, dt_bias, a_log, d_skip, norm_g, bsz, seq)
    return _mm_res_ln(y, w_out.astype(bf16), h, ln_g, ln_b)


def _moba_layer(h, bsz, seq, w_q, w_o, k, v, kmean, tables, ln_g, ln_b):
    q = _q_proj(h, w_q.astype(bf16), tables, seq)
    o = _moba(q, k, v, kmean, bsz, seq)
    return _mm_res_ln(o, w_o.astype(bf16), h, ln_g, ln_b)


def kernel(x, ssm_w_in, ssm_conv_w, ssm_conv_b, ssm_dt_bias, ssm_a_log, ssm_d, ssm_norm_g, ssm_w_out,
           kv_w_k, kv_w_v, attn_w_q, attn_w_o, moe_w_router, moe_b_router, moe_w_gate_up, moe_b_gate_up,
           moe_w_down, moe_b_down, ln_mix_g, ln_mix_b, ln_ffn_g, ln_ffn_b):
    bsz, seq, _ = x.shape
    h = x.reshape(bsz * seq, D_MODEL).astype(f32)
    tables = _rope_tables(seq)
    k = v = kmean = None
    for l in range(DEPTH):
        if l < N_A_LAYERS:
            h = _mamba_layer(h, bsz, seq, ssm_w_in[l], ssm_conv_w[l], ssm_conv_b[l], ssm_dt_bias[l], ssm_a_log[l],
                             ssm_d[l], ssm_norm_g[l], ssm_w_out[l], ln_mix_g[l], ln_mix_b[l])
        else:
            if l == N_A_LAYERS:
                k, v, kmean = _kv_proj(h, kv_w_k.astype(bf16), kv_w_v.astype(bf16), tables, seq)
                kmean = kmean.reshape(bsz, seq // MOBA_BLOCK, ATTN_DIM)
            j = l - N_A_LAYERS
            h = _moba_layer(h, bsz, seq, attn_w_q[j], attn_w_o[j], k, v, kmean, tables, ln_mix_g[l], ln_mix_b[l])
        h = _moe_layer(h, l, moe_w_router[l], moe_b_router[l], moe_w_gate_up, moe_b_gate_up[l],
                       moe_w_down, moe_b_down[l], ln_ffn_g[l], ln_ffn_b[l])
    return h.reshape(bsz, seq, D_MODEL).astype(x.dtype)
```

```python
import functools

import jax
import jax.numpy as jnp
from jax import lax
from jax.experimental import pallas as pl
from jax.experimental.pallas import tpu as pltpu

f32, bf16, i32 = jnp.float32, jnp.bfloat16, jnp.int32

D_MODEL = 1024
DEPTH = 4
N_A_LAYERS = DEPTH // 2

D_INNER = 2048
SSM_HEAD_DIM = 64
SSM_HEADS = D_INNER // SSM_HEAD_DIM
SSM_GROUPS = 4
SSM_STATE = 128
CONV_WIDTH = 4
SSM_CHUNK = 128
BC_DIM = SSM_GROUPS * SSM_STATE
CONV_DIM = D_INNER + 2 * BC_DIM
GROUP_COLS = D_INNER // SSM_GROUPS

ATTN_HEADS = 16
ATTN_HEAD_DIM = 64
ATTN_DIM = ATTN_HEADS * ATTN_HEAD_DIM
ROT_DIM = ATTN_HEAD_DIM // 4
ROPE_THETA = 500000.0
MOBA_BLOCK = 256
MOBA_TOPK = 3

N_EXPERTS = 32
TOP_K = 4
D_EXPERT = D_MODEL
SWIGLU_LIMIT = 7.0
SWIGLU_ALPHA = 1.702

LN_EPS = 1e-5
RMS_EPS = 1e-5
DEEPNORM_ALPHA = (2 * DEPTH) ** 0.25

LANES = 128
SUBLANES = 8
VMEM_LIMIT = 52 * 1024 * 1024

ROW_TILE = 512
MOE_TM = 256
ROUTER_TILE = 512
COMBINE_TILE = 128

NEG_INF = float("-inf")


def _nt(a, b):
    return lax.dot_general(a, b, (((1,), (1,)), ((), ())), preferred_element_type=f32)


def _tn(a, b):
    return lax.dot_general(a, b, (((0,), (0,)), ((), ())), preferred_element_type=f32)


def _dot(a, b):
    return jnp.dot(a, b, preferred_element_type=f32)


def _split2(x):
    hi = x.astype(bf16)
    lo = (x - hi.astype(f32)).astype(bf16)
    return hi, lo


def _split3(x):
    hi = x.astype(bf16)
    r = x - hi.astype(f32)
    mid = r.astype(bf16)
    lo = (r - mid.astype(f32)).astype(bf16)
    return hi, mid, lo


def _sigmoid(x):
    return 1.0 / (1.0 + jnp.exp(-x))


def _layer_norm(v, g, b):
    mu = jnp.mean(v, axis=-1, keepdims=True)
    vc = v - mu
    var = jnp.mean(vc * vc, axis=-1, keepdims=True)
    return vc * lax.rsqrt(var + LN_EPS) * g + b


def _params(*sem):
    return pltpu.CompilerParams(dimension_semantics=sem, vmem_limit_bytes=VMEM_LIMIT)


def _in_proj_body(h_ref, wz_ref, wx_ref, wdt_ref, z_ref, xbc_ref, dt_ref):
    x = h_ref[...].astype(bf16)
    for c in range(D_INNER // 1024):
        cs = slice(c * 1024, (c + 1) * 1024)
        z_ref[:, cs] = _dot(x, wz_ref[:, cs]).astype(bf16)
    for c in range(CONV_DIM // 1024):
        cs = slice(c * 1024, (c + 1) * 1024)
        xbc_ref[:, cs] = _dot(x, wx_ref[:, cs]).astype(bf16)
    dt_ref[...] = _dot(x, wdt_ref[...])


def _in_proj(h, wz, wx, wdt):
    n = h.shape[0]
    tm = ROW_TILE
    const = lambda i: (0, 0)
    return pl.pallas_call(
        _in_proj_body,
        grid=(n // tm,),
        in_specs=[pl.BlockSpec((tm, D_MODEL), lambda i: (i, 0)),
                  pl.BlockSpec((D_MODEL, D_INNER), const),
                  pl.BlockSpec((D_MODEL, CONV_DIM), const),
                  pl.BlockSpec((D_MODEL, LANES), const)],
        out_specs=[pl.BlockSpec((tm, D_INNER), lambda i: (i, 0)),
                   pl.BlockSpec((tm, CONV_DIM), lambda i: (i, 0)),
                   pl.BlockSpec((tm, LANES), lambda i: (i, 0))],
        out_shape=[jax.ShapeDtypeStruct((n, D_INNER), bf16),
                   jax.ShapeDtypeStruct((n, CONV_DIM), bf16),
                   jax.ShapeDtypeStruct((n, LANES), f32)],
        compiler_params=_params("parallel"),
        name="ssm_in_proj",
    )(h, wz, wx, wdt)


_EXT_PAD = SUBLANES


def _ssd_body(z_ref, xbc_ref, dt_ref, cw_ref, cb_ref, dtb_ref, alog_ref, dexp_ref, ng_ref, e_ref, tril_ref,
              y_ref, ext_scr, act_scr, st_scr, y_scr, ex_scr):
    c = pl.program_id(1)
    L = SSM_CHUNK

    @pl.when(c == 0)
    def _():
        ext_scr[0:_EXT_PAD, :] = jnp.zeros((_EXT_PAD, CONV_DIM), f32)
        st_scr[...] = jnp.zeros_like(st_scr)

    ext_scr[_EXT_PAD:_EXT_PAD + L, :] = xbc_ref[...].astype(f32)
    for cc in range(CONV_DIM // 512):
        cs = slice(cc * 512, (cc + 1) * 512)
        conv = cb_ref[:, cs] + cw_ref[CONV_WIDTH - 1:CONV_WIDTH, cs] * ext_scr[_EXT_PAD:_EXT_PAD + L, cs]
        for j in range(1, CONV_WIDTH):
            conv = conv + cw_ref[CONV_WIDTH - 1 - j:CONV_WIDTH - j, cs] * ext_scr[_EXT_PAD - j:_EXT_PAD - j + L, cs]
        act_scr[:, cs] = conv * _sigmoid(conv)
    ext_scr[0:_EXT_PAD, :] = ext_scr[L:L + _EXT_PAD, :]

    dtv = dt_ref[...] + dtb_ref[...]
    dt = jnp.maximum(dtv, 0.0) + jnp.log1p(jnp.exp(-jnp.abs(dtv)))
    a = -jnp.exp(alog_ref[...])
    adt = dt * a
    tril = tril_ref[...]
    a1, a2, a3 = _split3(adt)
    acs = _dot(tril, a1) + _dot(tril, a2) + _dot(tril, a3)
    acs_t = acs.T
    dt_t = dt.T
    a_last = acs[L - 1:L, :]
    dte = jnp.exp(a_last - acs)
    ea = jnp.exp(acs)
    cd = jnp.exp(a_last)
    stack = jnp.concatenate([dt * dte, ea, jnp.broadcast_to(cd, (SUBLANES, LANES))], axis=0)
    s_hi, s_lo = _split2(stack)
    ex_scr[...] = _dot(s_hi, e_ref[...]) + _dot(s_lo, e_ref[...])

    row = lax.broadcasted_iota(i32, (L, L), 0)
    col = lax.broadcasted_iota(i32, (L, L), 1)
    lower = row >= col
    lane = lax.broadcasted_iota(i32, (1, LANES), 1)
    first_head = lane < SSM_HEAD_DIM

    for g in range(SSM_GROUPS):
        gs = slice(g * GROUP_COLS, (g + 1) * GROUP_COLS)
        b_g = act_scr[:, D_INNER + g * SSM_STATE:D_INNER + (g + 1) * SSM_STATE].astype(bf16)
        c_g = act_scr[:, D_INNER + BC_DIM + g * SSM_STATE:D_INNER + BC_DIM + (g + 1) * SSM_STATE].astype(bf16)
        cb = _nt(c_g, b_g)
        st_old = st_scr[:, gs]
        xs_g = act_scr[:, gs]
        y_scr[:, gs] = _dot(c_g, st_old.astype(bf16)) * ex_scr[L:2 * L, gs] + dexp_ref[:, gs] * xs_g
        xw = (xs_g * ex_scr[0:L, gs]).astype(bf16)
        st_scr[:, gs] = st_old * ex_scr[2 * L:2 * L + 1, gs] + _tn(b_g, xw)
        for j in range(GROUP_COLS // LANES):
            h0 = g * (GROUP_COLS // SSM_HEAD_DIM) + 2 * j
            ms = []
            for h in (h0, h0 + 1):
                diff = acs[:, h:h + 1] - acs_t[h:h + 1, :]
                decay = jnp.exp(jnp.where(lower, diff, NEG_INF))
                ms.append((cb * decay * dt_t[h:h + 1, :]).astype(bf16))
            lhs = jnp.concatenate(ms, axis=1)
            ps = slice(g * GROUP_COLS + j * LANES, g * GROUP_COLS + (j + 1) * LANES)
            xp = act_scr[:, ps].astype(bf16)
            zero = jnp.zeros_like(xp)
            rhs = jnp.concatenate([jnp.where(first_head, xp, zero), jnp.where(first_head, zero, xp)], axis=0)
            y_scr[:, ps] += _dot(lhs, rhs)

    for g in range(SSM_GROUPS):
        gs = slice(g * GROUP_COLS, (g + 1) * GROUP_COLS)
        zf = z_ref[:, gs].astype(f32)
        hg = y_scr[:, gs] * (zf * _sigmoid(zf))
        ms = jnp.mean(hg * hg, axis=-1, keepdims=True)
        y_ref[:, gs] = (hg * lax.rsqrt(ms + RMS_EPS) * ng_ref[:, gs]).astype(bf16)


def _ssd(z, xbc, dt, conv_w, conv_b, dt_bias, a_log, d_skip, norm_g, bsz, seq):
    n = bsz * seq
    L = SSM_CHUNK
    nc = seq // L
    pad = LANES - SSM_HEADS
    dtb = jnp.pad(dt_bias.astype(f32), (0, pad)).reshape(1, LANES)
    alog = jnp.pad(a_log.astype(f32), (0, pad)).reshape(1, LANES)
    dexp = jnp.repeat(d_skip.astype(f32), SSM_HEAD_DIM).reshape(1, D_INNER)
    head_of_col = jnp.arange(D_INNER, dtype=i32) // SSM_HEAD_DIM
    expand = (jnp.arange(LANES, dtype=i32)[:, None] == head_of_col[None, :]).astype(bf16)
    tril = jnp.tril(jnp.ones((L, L), f32)).astype(bf16)
    tok = lambda b, c: (b * nc + c, 0)
    const = lambda b, c: (0, 0)
    ex_rows = 2 * L + SUBLANES
    return pl.pallas_call(
        _ssd_body,
        grid=(bsz, nc),
        in_specs=[pl.BlockSpec((L, D_INNER), tok),
                  pl.BlockSpec((L, CONV_DIM), tok),
                  pl.BlockSpec((L, LANES), tok),
                  pl.BlockSpec((CONV_WIDTH, CONV_DIM), const),
                  pl.BlockSpec((1, CONV_DIM), const),
                  pl.BlockSpec((1, LANES), const),
                  pl.BlockSpec((1, LANES), const),
                  pl.BlockSpec((1, D_INNER), const),
                  pl.BlockSpec((1, D_INNER), const),
                  pl.BlockSpec((LANES, D_INNER), const),
                  pl.BlockSpec((L, L), const)],
        out_specs=pl.BlockSpec((L, D_INNER), tok),
        out_shape=jax.ShapeDtypeStruct((n, D_INNER), bf16),
        scratch_shapes=[pltpu.VMEM((L + 2 * _EXT_PAD, CONV_DIM), f32),
                        pltpu.VMEM((L, CONV_DIM), f32),
                        pltpu.VMEM((SSM_STATE, D_INNER), f32),
                        pltpu.VMEM((L, D_INNER), f32),
                        pltpu.VMEM((ex_rows, D_INNER), f32)],
        compiler_params=_params("parallel", "arbitrary"),
        name="ssm_ssd",
    )(z, xbc, dt, conv_w.astype(f32), conv_b.astype(f32).reshape(1, CONV_DIM), dtb, alog, dexp,
      norm_g.astype(f32).reshape(1, D_INNER), expand, tril)


def _mm_res_ln_body(y_ref, w_ref, h_ref, g_ref, b_ref, o_ref):
    mix = _dot(y_ref[...], w_ref[...])
    o_ref[...] = _layer_norm(DEEPNORM_ALPHA * h_ref[...] + mix, g_ref[...], b_ref[...])


def _mm_res_ln(y, w, h, g, b):
    n, k = y.shape
    tm = ROW_TILE
    const = lambda i: (0, 0)
    return pl.pallas_call(
        _mm_res_ln_body,
        grid=(n // tm,),
        in_specs=[pl.BlockSpec((tm, k), lambda i: (i, 0)),
                  pl.BlockSpec((k, D_MODEL), const),
                  pl.BlockSpec((tm, D_MODEL), lambda i: (i, 0)),
                  pl.BlockSpec((1, D_MODEL), const),
                  pl.BlockSpec((1, D_MODEL), const)],
        out_specs=pl.BlockSpec((tm, D_MODEL), lambda i: (i, 0)),
        out_shape=jax.ShapeDtypeStruct((n, D_MODEL), f32),
        compiler_params=_params("parallel"),
        name="out_proj_ln",
    )(y, w, h, g.astype(f32).reshape(1, D_MODEL), b.astype(f32).reshape(1, D_MODEL))


def _rope_tables(seq):
    half = ROT_DIM // 2
    inv_freq = ROPE_THETA ** (-jnp.arange(0, ROT_DIM, 2, dtype=f32) / ROT_DIM)
    ang = jnp.arange(seq, dtype=f32)[:, None] * inv_freq[None, :]
    cos, sin = jnp.cos(ang), jnp.sin(ang)
    j = jnp.arange(LANES) % ATTN_HEAD_DIM
    idx = j % half
    first = (j < half)[None, :]
    second = ((j >= half) & (j < ROT_DIM))[None, :]
    cos_t = jnp.where(first | second, cos[:, idx], 1.0)
    sin_a = jnp.where(first, -sin[:, idx], 0.0)
    sin_b = jnp.where(second, sin[:, idx], 0.0)
    return cos_t, sin_a, sin_b


def _rope_cols(x, cos_t, sin_a, sin_b):
    half = ROT_DIM // 2
    return x * cos_t + pltpu.roll(x, LANES - half, axis=1) * sin_a + pltpu.roll(x, half, axis=1) * sin_b


def _q_proj_body(h_ref, w_ref, cos_ref, sa_ref, sb_ref, q_ref):
    x = h_ref[...].astype(bf16)
    scale = ATTN_HEAD_DIM ** -0.5
    for c in range(ATTN_DIM // LANES):
        cs = slice(c * LANES, (c + 1) * LANES)
        q = _rope_cols(_dot(x, w_ref[:, cs]), cos_ref[...], sa_ref[...], sb_ref[...])
        q_ref[:, cs] = (q * scale).astype(bf16)


def _q_proj(h, w, tables, seq):
    n = h.shape[0]
    tm = ROW_TILE
    per_seq = seq // tm
    const = lambda i: (0, 0)
    pos = lambda i: (i % per_seq, 0)
    return pl.pallas_call(
        _q_proj_body,
        grid=(n // tm,),
        in_specs=[pl.BlockSpec((tm, D_MODEL), lambda i: (i, 0)),
                  pl.BlockSpec((D_MODEL, ATTN_DIM), const),
                  pl.BlockSpec((tm, LANES), pos), pl.BlockSpec((tm, LANES), pos), pl.BlockSpec((tm, LANES), pos)],
        out_specs=pl.BlockSpec((tm, ATTN_DIM), lambda i: (i, 0)),
        out_shape=jax.ShapeDtypeStruct((n, ATTN_DIM), bf16),
        compiler_params=_params("parallel"),
        name="attn_q_proj",
    )(h, w, *tables)


def _kv_proj_body(h_ref, wk_ref, wv_ref, cos_ref, sa_ref, sb_ref, k_ref, v_ref, km_ref):
    x = h_ref[...].astype(bf16)
    for c in range(ATTN_DIM // LANES):
        cs = slice(c * LANES, (c + 1) * LANES)
        k = _rope_cols(_dot(x, wk_ref[:, cs]), cos_ref[...], sa_ref[...], sb_ref[...])
        k_ref[:, cs] = k.astype(bf16)
        km_ref[0, :, cs] = jnp.mean(k, axis=0, keepdims=True)
        v_ref[:, cs] = _dot(x, wv_ref[:, cs]).astype(bf16)


def _kv_proj(h, wk, wv, tables, seq):
    n = h.shape[0]
    tm = MOBA_BLOCK
    per_seq = seq // tm
    const = lambda i: (0, 0)
    pos = lambda i: (i % per_seq, 0)
    return pl.pallas_call(
        _kv_proj_body,
        grid=(n // tm,),
        in_specs=[pl.BlockSpec((tm, D_MODEL), lambda i: (i, 0)),
                  pl.BlockSpec((D_MODEL, ATTN_DIM), const),
                  pl.BlockSpec((D_MODEL, ATTN_DIM), const),
                  pl.BlockSpec((tm, LANES), pos), pl.BlockSpec((tm, LANES), pos), pl.BlockSpec((tm, LANES), pos)],
        out_specs=[pl.BlockSpec((tm, ATTN_DIM), lambda i: (i, 0)),
                   pl.BlockSpec((tm, ATTN_DIM), lambda i: (i, 0)),
                   pl.BlockSpec((1, 1, ATTN_DIM), lambda i: (i, 0, 0))],
        out_shape=[jax.ShapeDtypeStruct((n, ATTN_DIM), bf16),
                   jax.ShapeDtypeStruct((n, ATTN_DIM), bf16),
                   jax.ShapeDtypeStruct((n // tm, 1, ATTN_DIM), f32)],
        compiler_params=_params("parallel"),
        name="attn_kv_proj",
    )(h, wk, wv, *tables)


def _moba_body(q_ref, k_ref, v_ref, km_ref, o_ref, *, nb):
    qb = pl.program_id(2)
    blk = MOBA_BLOCK
    q = q_ref[...]
    lane = lax.broadcasted_iota(i32, (1, LANES), 1)
    first_head = lane < ATTN_HEAD_DIM
    zero = jnp.zeros_like(q)
    q2 = jnp.concatenate([jnp.where(first_head, q, zero), jnp.where(first_head, zero, q)], axis=0)
    row = lax.broadcasted_iota(i32, (2 * blk, blk), 0)
    col = lax.broadcasted_iota(i32, (2 * blk, blk), 1)
    causal = (col <= row) & ((row < blk) | (col + blk <= row))
    eye = (lax.broadcasted_iota(i32, (blk, blk), 0) == lax.broadcasted_iota(i32, (blk, blk), 1)).astype(bf16)
    blk_id = lax.broadcasted_iota(i32, (nb, 2 * blk), 0)
    km_hi, km_lo = _split2(km_ref[...])
    past = blk_id < qb
    gate = jnp.where(past, _nt(km_hi, q2) + _nt(km_lo, q2), NEG_INF)
    rank = jnp.zeros((nb, 2 * blk), f32)
    for j2 in range(nb):
        gj = gate[j2:j2 + 1, :]
        beats = (gj > gate) | ((gj == gate) & (blk_id > j2))
        rank = rank + beats.astype(f32)
    sel_t = (past & (rank < MOBA_TOPK)).astype(bf16)
    sel = jnp.concatenate([_nt(eye, sel_t[:, 0:blk]), _nt(eye, sel_t[:, blk:2 * blk])], axis=0)

    for c in range(nb):
        @pl.when(qb == c)
        def _(c=c):
            w = (c + 1) * blk
            s = _nt(q2, k_ref[0:w, :])
            parts = [jnp.where(sel[:, j:j + 1] > 0.5, s[:, j * blk:(j + 1) * blk], NEG_INF) for j in range(c)]
            parts.append(jnp.where(causal, s[:, c * blk:w], NEG_INF))
            s = jnp.concatenate(parts, axis=1) if c else parts[0]
            m = jnp.max(s, axis=-1, keepdims=True)
            p = jnp.exp(s - m)
            l = jnp.sum(p, axis=-1, keepdims=True)
            o = _dot(p.astype(bf16), v_ref[0:w, :]) / l
            o_ref[...] = jnp.where(first_head, o[0:blk, :], o[blk:2 * blk, :]).astype(bf16)


def _moba(q, k, v, kmean, bsz, seq):
    n = bsz * seq
    nb = seq // MOBA_BLOCK
    n_pairs = ATTN_DIM // LANES
    return pl.pallas_call(
        functools.partial(_moba_body, nb=nb),
        grid=(bsz, n_pairs, nb),
        in_specs=[pl.BlockSpec((MOBA_BLOCK, LANES), lambda b, hp, qb: (b * nb + qb, hp)),
                  pl.BlockSpec((seq, LANES), lambda b, hp, qb: (b, hp)),
                  pl.BlockSpec((seq, LANES), lambda b, hp, qb: (b, hp)),
                  pl.BlockSpec((None, nb, LANES), lambda b, hp, qb: (b, 0, hp))],
        out_specs=pl.BlockSpec((MOBA_BLOCK, LANES), lambda b, hp, qb: (b * nb + qb, hp)),
        out_shape=jax.ShapeDtypeStruct((n, ATTN_DIM), bf16),
        compiler_params=_params("parallel", "parallel", "arbitrary"),
        name="moba_attention",
    )(q, k, v, kmean)


def _router_body(h_ref, wt_ref, b_ref, u_ref, idx_ref, gate_ref, rank_ref, cnt_ref, cnt_scr):
    i = pl.program_id(0)

    @pl.when(i == 0)
    def _():
        cnt_scr[...] = jnp.zeros_like(cnt_scr)

    tr = h_ref.shape[0]
    x_hi, x_lo = _split2(h_ref[...])
    w_hi, w_lo = _split2(wt_ref[...])
    logits = _nt(w_hi, x_hi) + _nt(w_hi, x_lo) + _nt(w_lo, x_hi) + b_ref[...]
    e_id = lax.broadcasted_iota(i32, (N_EXPERTS, tr), 0)
    cur = logits
    vals, idxs = [], []
    for _ in range(TOP_K):
        m = jnp.max(cur, axis=0, keepdims=True)
        idx = jnp.min(jnp.where(cur == m, e_id, N_EXPERTS), axis=0, keepdims=True)
        vals.append(m)
        idxs.append(idx)
        cur = jnp.where(e_id == idx, NEG_INF, cur)
    exps = [jnp.exp(v - vals[0]) for v in vals]
    denom = exps[0]
    for e in exps[1:]:
        denom = denom + e
    onehot = jnp.zeros((N_EXPERTS, tr), f32)
    for idx in idxs:
        onehot = onehot + (e_id == idx).astype(f32)
    base = cnt_scr[:, 0:1]
    before = _dot(onehot.astype(bf16), u_ref[...]) + base
    gate_ref[...] = jnp.zeros_like(gate_ref)
    for r in range(TOP_K):
        idx_ref[r:r + 1, :] = idxs[r]
        gate_ref[r:r + 1, :] = exps[r] / denom
        rank_ref[r:r + 1, :] = jnp.sum(jnp.where(e_id == idxs[r], before, 0.0), axis=0, keepdims=True).astype(i32)
    total = base + jnp.sum(onehot, axis=1, keepdims=True)
    cnt_scr[...] = jnp.broadcast_to(total, cnt_scr.shape)
    cnt_ref[...] = jnp.broadcast_to(total, cnt_ref.shape).astype(i32)


def _router(h, w_router, b_router):
    n = h.shape[0]
    tr = ROUTER_TILE
    wt = w_router.astype(f32).T
    b = b_router.astype(f32).reshape(N_EXPERTS, 1)
    strict_upper = (jnp.arange(tr)[:, None] < jnp.arange(tr)[None, :]).astype(bf16)
    const = lambda i: (0, 0)
    tok = lambda i: (0, i)
    return pl.pallas_call(
        _router_body,
        grid=(n // tr,),
        in_specs=[pl.BlockSpec((tr, D_MODEL), lambda i: (i, 0)),
                  pl.BlockSpec((N_EXPERTS, D_MODEL), const),
                  pl.BlockSpec((N_EXPERTS, 1), const),
                  pl.BlockSpec((tr, tr), const)],
        out_specs=[pl.BlockSpec((TOP_K, tr), tok),
                   pl.BlockSpec((SUBLANES, tr), tok),
                   pl.BlockSpec((TOP_K, tr), tok),
                   pl.BlockSpec((N_EXPERTS, LANES), const)],
        out_shape=[jax.ShapeDtypeStruct((TOP_K, n), i32),
                   jax.ShapeDtypeStruct((SUBLANES, n), f32),
                   jax.ShapeDtypeStruct((TOP_K, n), i32),
                   jax.ShapeDtypeStruct((N_EXPERTS, LANES), i32)],
        scratch_shapes=[pltpu.VMEM((N_EXPERTS, LANES), f32)],
        compiler_params=_params("arbitrary"),
        name="moe_router",
    )(h, wt, b, strict_upper)


def _gather_rows(idx_smem, src_hbm, dst_vmem, sem, n_rows, idx_base=0, both_queues=False):
    for r in range(n_rows):
        tok = idx_smem[idx_base + r]
        copy = pltpu.make_async_copy(src_hbm.at[pl.ds(tok, 1)], dst_vmem.at[pl.ds(r, 1)], sem)
        copy.start(priority=r % 2 if both_queues else 0)


def _wait_rows(src_hbm, dst_vmem, sem, n_rows):
    pltpu.make_async_copy(src_hbm.at[pl.ds(0, n_rows)], dst_vmem, sem).wait()


def _idx_copy(idx_hbm, idx_smem, idx_sem, tile, slot, n_idx):
    src = idx_hbm.at[pl.ds(pl.multiple_of(tile * n_idx, n_idx), n_idx)]
    dst = idx_smem.at[pl.ds(pl.multiple_of(slot * n_idx, n_idx), n_idx)]
    return pltpu.make_async_copy(src, dst, idx_sem.at[slot])


def _expert_body(blk_e_ref, nact_ref, first_ref, tok_hbm, h_hbm, wgu_ref, bgu_ref, wd_ref, bd_ref, y_ref,
                 idx_smem, x_buf, wgu_bf, wd_bf, idx_sem, row_sem):
    i = pl.program_id(0)
    tm = x_buf.shape[1]
    nact = nact_ref[0]
    last_blk = pl.num_programs(0) - 1
    slot = i % 2
    nxt = 1 - slot

    @pl.when(i == 0)
    def _():
        first = _idx_copy(tok_hbm, idx_smem, idx_sem, 0, 0, tm)
        first.start()
        first.wait()
        _gather_rows(idx_smem, h_hbm, x_buf.at[0], row_sem.at[0], tm, both_queues=True)
        _idx_copy(tok_hbm, idx_smem, idx_sem, jnp.minimum(1, last_blk), 1, tm).start()

    @pl.when(i < nact)
    def _():
        @pl.when(i + 1 < nact)
        def _():
            _idx_copy(tok_hbm, idx_smem, idx_sem, jnp.minimum(i + 2, last_blk), slot, tm).start()

        _idx_copy(tok_hbm, idx_smem, idx_sem, jnp.minimum(i + 1, last_blk), nxt, tm).wait()
        _wait_rows(h_hbm, x_buf.at[slot], row_sem.at[slot], tm)
        _gather_rows(idx_smem, h_hbm, x_buf.at[nxt], row_sem.at[nxt], tm, idx_base=nxt * tm, both_queues=True)

        @pl.when(first_ref[i] == 1)
        def _():
            for c in range(2 * D_EXPERT // 512):
                cs = slice(c * 512, (c + 1) * 512)
                wgu_bf[:, cs] = wgu_ref[:, cs].astype(bf16)
            for c in range(D_MODEL // 512):
                cs = slice(c * 512, (c + 1) * 512)
                wd_bf[:, cs] = wd_ref[:, cs].astype(bf16)

        x = x_buf[slot].astype(bf16)
        gu = _dot(x, wgu_bf[...]) + bgu_ref[...]
        g_lin = jnp.minimum(gu[:, :D_EXPERT], SWIGLU_LIMIT)
        u_lin = jnp.clip(gu[:, D_EXPERT:], -SWIGLU_LIMIT, SWIGLU_LIMIT)
        act = (u_lin + 1.0) * (g_lin * _sigmoid(g_lin * SWIGLU_ALPHA))
        y_ref[...] = _dot(act.astype(bf16), wd_bf[...]) + bd_ref[...]

        @pl.when(i + 1 == nact)
        def _():
            _wait_rows(h_hbm, x_buf.at[nxt], row_sem.at[nxt], tm)

    @pl.when(i >= nact)
    def _():
        y_ref[...] = jnp.zeros_like(y_ref)


def _expert_ffn(h, row_tok, blk_e, nact, first, layer, w_gu, b_gu, w_down, b_down, n_blocks):
    tm = MOE_TM
    w_map = lambda i, be, na, fi: (layer, be[i], 0, 0)
    b_map = lambda i, be, na, fi: (be[i], 0, 0)
    out_map = lambda i, be, na, fi: (i, 0)
    return pl.pallas_call(
        _expert_body,
        grid_spec=pltpu.PrefetchScalarGridSpec(
            num_scalar_prefetch=3,
            grid=(n_blocks,),
            in_specs=[pl.BlockSpec(memory_space=pl.ANY),
                      pl.BlockSpec(memory_space=pl.ANY),
                      pl.BlockSpec((None, None, D_MODEL, 2 * D_EXPERT), w_map),
                      pl.BlockSpec((None, 1, 2 * D_EXPERT), b_map),
                      pl.BlockSpec((None, None, D_EXPERT, D_MODEL), w_map),
                      pl.BlockSpec((None, 1, D_MODEL), b_map)],
            out_specs=pl.BlockSpec((tm, D_MODEL), out_map),
            scratch_shapes=[pltpu.SMEM((2 * tm,), i32),
                            pltpu.VMEM((2, tm, D_MODEL), f32),
                            pltpu.VMEM((D_MODEL, 2 * D_EXPERT), bf16),
                            pltpu.VMEM((D_EXPERT, D_MODEL), bf16),
                            pltpu.SemaphoreType.DMA((2,)),
                            pltpu.SemaphoreType.DMA((2,))]),
        out_shape=jax.ShapeDtypeStruct((n_blocks * tm, D_MODEL), f32),
        compiler_params=_params("arbitrary"),
        name="moe_experts",
    )(blk_e, nact, first, row_tok, h, w_gu, b_gu, w_down, b_down)


def _combine_body(dest_hbm, ybuf_hbm, gate_ref, h_ref, g_ref, b_ref, o_ref, idx_smem, rows, idx_sem, row_sem):
    i = pl.program_id(0)
    tc = h_ref.shape[0]
    n_idx = TOP_K * tc
    last = pl.num_programs(0) - 1
    slot = i % 2
    nxt = 1 - slot

    def gather(s):
        for k in range(TOP_K):
            _gather_rows(idx_smem, ybuf_hbm, rows.at[s, k], row_sem.at[s], tc, idx_base=s * n_idx + k * tc,
                         both_queues=True)

    def wait(s):
        for k in range(TOP_K):
            _wait_rows(ybuf_hbm, rows.at[s, k], row_sem.at[s], tc)

    @pl.when(i == 0)
    def _():
        first = _idx_copy(dest_hbm, idx_smem, idx_sem, 0, 0, n_idx)
        first.start()
        first.wait()
        gather(0)
        _idx_copy(dest_hbm, idx_smem, idx_sem, jnp.minimum(1, last), 1, n_idx).start()

    @pl.when(i < last)
    def _():
        _idx_copy(dest_hbm, idx_smem, idx_sem, jnp.minimum(i + 2, last), slot, n_idx).start()

    _idx_copy(dest_hbm, idx_smem, idx_sem, jnp.minimum(i + 1, last), nxt, n_idx).wait()
    wait(slot)
    gather(nxt)
    gates = jnp.concatenate([gate_ref[...], jnp.zeros((tc - SUBLANES, tc), f32)], axis=0).T
    moe = rows[slot, 0] * gates[:, 0:1]
    for k in range(1, TOP_K):
        moe = moe + rows[slot, k] * gates[:, k:k + 1]
    o_ref[...] = _layer_norm(DEEPNORM_ALPHA * h_ref[...] + moe, g_ref[...], b_ref[...])

    @pl.when(i == last)
    def _():
        wait(nxt)


def _combine_ln(dest_tiles, ybuf, gates, h, g, b):
    n = h.shape[0]
    tc = COMBINE_TILE
    const = lambda i: (0, 0)
    return pl.pallas_call(
        _combine_body,
        grid=(n // tc,),
        in_specs=[pl.BlockSpec(memory_space=pl.ANY),
                  pl.BlockSpec(memory_space=pl.ANY),
                  pl.BlockSpec((SUBLANES, tc), lambda i: (0, i)),
                  pl.BlockSpec((tc, D_MODEL), lambda i: (i, 0)),
                  pl.BlockSpec((1, D_MODEL), const),
                  pl.BlockSpec((1, D_MODEL), const)],
        out_specs=pl.BlockSpec((tc, D_MODEL), lambda i: (i, 0)),
        out_shape=jax.ShapeDtypeStruct((n, D_MODEL), f32),
        scratch_shapes=[pltpu.SMEM((2 * TOP_K * tc,), i32),
                        pltpu.VMEM((2, TOP_K, tc, D_MODEL), f32),
                        pltpu.SemaphoreType.DMA((2,)),
                        pltpu.SemaphoreType.DMA((2,))],
        compiler_params=_params("arbitrary"),
        name="moe_combine_ln",
    )(dest_tiles, ybuf, gates, h, g.astype(f32).reshape(1, D_MODEL), b.astype(f32).reshape(1, D_MODEL))


def _moe_layer(h, layer, w_router, b_router, w_gu, b_gu, w_down, b_down, ln_g, ln_b):
    n = h.shape[0]
    tm = MOE_TM
    n_blocks = (n * TOP_K) // tm + N_EXPERTS
    idx, gates, rank, counts = _router(h, w_router, b_router)
    counts = counts[:, 0]
    blocks_per_e = (counts + tm - 1) // tm
    blk_end = jnp.cumsum(blocks_per_e)
    blk_start = blk_end - blocks_per_e
    nact = blk_end[-1:].astype(i32)
    e_ids = jnp.arange(N_EXPERTS, dtype=i32)
    start_of = jnp.sum(jnp.where(idx[:, :, None] == e_ids, blk_start, 0), axis=-1)
    dest = start_of * tm + rank
    blk_ids = jnp.minimum(jnp.arange(n_blocks, dtype=i32), nact[0] - 1)
    blk_e = jnp.minimum(jnp.sum(blk_end[None, :] <= blk_ids[:, None], axis=-1), N_EXPERTS - 1).astype(i32)
    tok_ids = jnp.broadcast_to(jnp.arange(n, dtype=i32)[None, :], (TOP_K, n))
    row_tok = jnp.zeros((n_blocks * tm,), i32).at[dest.reshape(-1)].set(tok_ids.reshape(-1), unique_indices=True)
    first = jnp.concatenate([jnp.ones((1,), i32), (blk_e[1:] != blk_e[:-1]).astype(i32)])
    ybuf = _expert_ffn(h, row_tok, blk_e, nact, first, layer, w_gu,
                       b_gu.astype(f32).reshape(N_EXPERTS, 1, 2 * D_EXPERT),
                       w_down, b_down.astype(f32).reshape(N_EXPERTS, 1, D_MODEL), n_blocks)
    tc = COMBINE_TILE
    dest_tiles = dest.reshape(TOP_K, n // tc, tc).transpose(1, 0, 2).reshape(-1)
    return _combine_ln(dest_tiles, ybuf, gates, h, ln_g, ln_b)


def _mamba_layer(h, bsz, seq, w_in, conv_w, conv_b, dt_bias, a_log, d_skip, norm_g, w_out, ln_g, ln_b):
    wz = w_in[:, :D_INNER].astype(bf16)
    wx = w_in[:, D_INNER:D_INNER + CONV_DIM].astype(bf16)
    wdt = jnp.pad(w_in[:, D_INNER + CONV_DIM:], ((0, 0), (0, LANES - SSM_HEADS))).astype(bf16)
    z, xbc, dt = _in_proj(h, wz, wx, wdt)
    y = _ssd(z, xbc, dt, conv_w, conv_b, dt_bias, a_log, d_skip, norm_g, bsz, seq)
    return _mm_res_ln(y, w_out.astype(bf16), h, ln_g, ln_b)


def _moba_layer(h, bsz, seq, w_q, w_o, k, v, kmean, tables, ln_g, ln_b):
    q = _q_proj(h, w_q.astype(bf16), tables, seq)
    o = _moba(q, k, v, kmean, bsz, seq)
    return _mm_res_ln(o, w_o.astype(bf16), h, ln_g, ln_b)


def kernel(x, ssm_w_in, ssm_conv_w, ssm_conv_b, ssm_dt_bias, ssm_a_log, ssm_d, ssm_norm_g, ssm_w_out,
           kv_w_k, kv_w_v, attn_w_q, attn_w_o, moe_w_router, moe_b_router, moe_w_gate_up, moe_b_gate_up,
           moe_w_down, moe_b_down, ln_mix_g, ln_mix_b, ln_ffn_g, ln_ffn_b):
    bsz, seq, _ = x.shape
    h = x.reshape(bsz * seq, D_MODEL).astype(f32)
    tables = _rope_tables(seq)
    k = v = kmean = None
    for l in range(DEPTH):
        if l < N_A_LAYERS:
            h = _mamba_layer(h, bsz, seq, ssm_w_in[l], ssm_conv_w[l], ssm_conv_b[l], ssm_dt_bias[l], ssm_a_log[l],
                             ssm_d[l], ssm_norm_g[l], ssm_w_out[l], ln_mix_g[l], ln_mix_b[l])
        else:
            if l == N_A_LAYERS:
                k, v, kmean = _kv_proj(h, kv_w_k.astype(bf16), kv_w_v.astype(bf16), tables, seq)
                kmean = kmean.reshape(bsz, seq // MOBA_BLOCK, ATTN_DIM)
            j = l - N_A_LAYERS
            h = _moba_layer(h, bsz, seq, attn_w_q[j], attn_w_o[j], k, v, kmean, tables, ln_mix_g[l], ln_mix_b[l])
        h = _moe_layer(h, l, moe_w_router[l], moe_b_router[l], moe_w_gate_up, moe_b_gate_up[l],
                       moe_w_down, moe_b_down[l], ln_ffn_g[l], ln_ffn_b[l])
    return h.reshape(bsz, seq, D_MODEL).astype(x.dtype)
```

```python
import functools

import jax
import jax.numpy as jnp
from jax import lax
from jax.experimental import pallas as pl
from jax.experimental.pallas import tpu as pltpu

f32, bf16, i32 = jnp.float32, jnp.bfloat16, jnp.int32

D_MODEL = 1024
DEPTH = 4
N_A_LAYERS = DEPTH // 2

D_INNER = 2048
SSM_HEAD_DIM = 64
SSM_HEADS = D_INNER // SSM_HEAD_DIM
SSM_GROUPS = 4
SSM_STATE = 128
CONV_WIDTH = 4
SSM_CHUNK = 128
BC_DIM = SSM_GROUPS * SSM_STATE
CONV_DIM = D_INNER + 2 * BC_DIM
GROUP_COLS = D_INNER // SSM_GROUPS

ATTN_HEADS = 16
ATTN_HEAD_DIM = 64
ATTN_DIM = ATTN_HEADS * ATTN_HEAD_DIM
ROT_DIM = ATTN_HEAD_DIM // 4
ROPE_THETA = 500000.0
MOBA_BLOCK = 256
MOBA_TOPK = 3

N_EXPERTS = 32
TOP_K = 4
D_EXPERT = D_MODEL
SWIGLU_LIMIT = 7.0
SWIGLU_ALPHA = 1.702

LN_EPS = 1e-5
RMS_EPS = 1e-5
DEEPNORM_ALPHA = (2 * DEPTH) ** 0.25

LANES = 128
SUBLANES = 8
VMEM_LIMIT = 52 * 1024 * 1024

ROW_TILE = 512
MOE_TM = 512
ROUTER_TILE = 512
COMBINE_TILE = 128

NEG_INF = float("-inf")


def _nt(a, b):
    return lax.dot_general(a, b, (((1,), (1,)), ((), ())), preferred_element_type=f32)


def _tn(a, b):
    return lax.dot_general(a, b, (((0,), (0,)), ((), ())), preferred_element_type=f32)


def _dot(a, b):
    return jnp.dot(a, b, preferred_element_type=f32)


def _split2(x):
    hi = x.astype(bf16)
    lo = (x - hi.astype(f32)).astype(bf16)
    return hi, lo


def _split3(x):
    hi = x.astype(bf16)
    r = x - hi.astype(f32)
    mid = r.astype(bf16)
    lo = (r - mid.astype(f32)).astype(bf16)
    return hi, mid, lo


def _sigmoid(x):
    return 0.5 * (jnp.tanh(0.5 * x) + 1.0)


def _layer_norm(v, g, b):
    mu = jnp.mean(v, axis=-1, keepdims=True)
    vc = v - mu
    var = jnp.mean(vc * vc, axis=-1, keepdims=True)
    return vc * lax.rsqrt(var + LN_EPS) * g + b


def _params(*sem):
    return pltpu.CompilerParams(dimension_semantics=sem, vmem_limit_bytes=VMEM_LIMIT)


def _in_proj_body(h_ref, wz_ref, wx_ref, wdt_ref, z_ref, xbc_ref, dt_ref):
    x = h_ref[...].astype(bf16)
    for c in range(D_INNER // 1024):
        cs = slice(c * 1024, (c + 1) * 1024)
        z_ref[:, cs] = _dot(x, wz_ref[:, cs]).astype(bf16)
    for c in range(CONV_DIM // 1024):
        cs = slice(c * 1024, (c + 1) * 1024)
        xbc_ref[:, cs] = _dot(x, wx_ref[:, cs]).astype(bf16)
    dt_ref[...] = _dot(x, wdt_ref[...])


def _in_proj(h, wz, wx, wdt):
    n = h.shape[0]
    tm = ROW_TILE
    const = lambda i: (0, 0)
    return pl.pallas_call(
        _in_proj_body,
        grid=(n // tm,),
        in_specs=[pl.BlockSpec((tm, D_MODEL), lambda i: (i, 0)),
                  pl.BlockSpec((D_MODEL, D_INNER), const),
                  pl.BlockSpec((D_MODEL, CONV_DIM), const),
                  pl.BlockSpec((D_MODEL, LANES), const)],
        out_specs=[pl.BlockSpec((tm, D_INNER), lambda i: (i, 0)),
                   pl.BlockSpec((tm, CONV_DIM), lambda i: (i, 0)),
                   pl.BlockSpec((tm, LANES), lambda i: (i, 0))],
        out_shape=[jax.ShapeDtypeStruct((n, D_INNER), bf16),
                   jax.ShapeDtypeStruct((n, CONV_DIM), bf16),
                   jax.ShapeDtypeStruct((n, LANES), f32)],
        compiler_params=_params("parallel"),
        name="ssm_in_proj",
    )(h, wz, wx, wdt)


_EXT_PAD = SUBLANES


def _ssd_body(z_ref, xbc_ref, dt_ref, cw_ref, cb_ref, dtb_ref, alog_ref, dexp_ref, ng_ref, e_ref, tril_ref,
              y_ref, ext_scr, act_scr, st_scr, y_scr, ex_scr):
    c = pl.program_id(1)
    L = SSM_CHUNK

    @pl.when(c == 0)
    def _():
        ext_scr[0:_EXT_PAD, :] = jnp.zeros((_EXT_PAD, CONV_DIM), f32)
        st_scr[...] = jnp.zeros_like(st_scr)

    ext_scr[_EXT_PAD:_EXT_PAD + L, :] = xbc_ref[...].astype(f32)
    for cc in range(CONV_DIM // 512):
        cs = slice(cc * 512, (cc + 1) * 512)
        conv = cb_ref[:, cs] + cw_ref[CONV_WIDTH - 1:CONV_WIDTH, cs] * ext_scr[_EXT_PAD:_EXT_PAD + L, cs]
        for j in range(1, CONV_WIDTH):
            conv = conv + cw_ref[CONV_WIDTH - 1 - j:CONV_WIDTH - j, cs] * ext_scr[_EXT_PAD - j:_EXT_PAD - j + L, cs]
        act_scr[:, cs] = conv * _sigmoid(conv)
    ext_scr[0:_EXT_PAD, :] = ext_scr[L:L + _EXT_PAD, :]

    dtv = dt_ref[...] + dtb_ref[...]
    dt = jnp.maximum(dtv, 0.0) + jnp.log1p(jnp.exp(-jnp.abs(dtv)))
    a = -jnp.exp(alog_ref[...])
    adt = dt * a
    tril = tril_ref[...]
    a1, a2, a3 = _split3(adt)
    acs = _dot(tril, a1) + _dot(tril, a2) + _dot(tril, a3)
    acs_t = acs.T
    dt_t = dt.T
    a_last = acs[L - 1:L, :]
    dte = jnp.exp(a_last - acs)
    ea = jnp.exp(acs)
    cd = jnp.exp(a_last)
    stack = jnp.concatenate([dt * dte, ea, jnp.broadcast_to(cd, (SUBLANES, LANES))], axis=0)
    s_hi, s_lo = _split2(stack)
    ex_scr[...] = _dot(s_hi, e_ref[...]) + _dot(s_lo, e_ref[...])

    row = lax.broadcasted_iota(i32, (L, L), 0)
    col = lax.broadcasted_iota(i32, (L, L), 1)
    lower = row >= col
    lane = lax.broadcasted_iota(i32, (1, LANES), 1)
    first_head = lane < SSM_HEAD_DIM

    for g in range(SSM_GROUPS):
        gs = slice(g * GROUP_COLS, (g + 1) * GROUP_COLS)
        b_g = act_scr[:, D_INNER + g * SSM_STATE:D_INNER + (g + 1) * SSM_STATE].astype(bf16)
        c_g = act_scr[:, D_INNER + BC_DIM + g * SSM_STATE:D_INNER + BC_DIM + (g + 1) * SSM_STATE].astype(bf16)
        cb = _nt(c_g, b_g)
        st_old = st_scr[:, gs]
        xs_g = act_scr[:, gs]
        y_scr[:, gs] = _dot(c_g, st_old.astype(bf16)) * ex_scr[L:2 * L, gs] + dexp_ref[:, gs] * xs_g
        xw = (xs_g * ex_scr[0:L, gs]).astype(bf16)
        st_scr[:, gs] = st_old * ex_scr[2 * L:2 * L + 1, gs] + _tn(b_g, xw)
        for j in range(GROUP_COLS // LANES):
            h0 = g * (GROUP_COLS // SSM_HEAD_DIM) + 2 * j
            ms = []
            for h in (h0, h0 + 1):
                diff = acs[:, h:h + 1] - acs_t[h:h + 1, :]
                decay = jnp.exp(jnp.where(lower, diff, NEG_INF))
                ms.append((cb * decay * dt_t[h:h + 1, :]).astype(bf16))
            lhs = jnp.concatenate(ms, axis=1)
            ps = slice(g * GROUP_COLS + j * LANES, g * GROUP_COLS + (j + 1) * LANES)
            xp = act_scr[:, ps].astype(bf16)
            zero = jnp.zeros_like(xp)
            rhs = jnp.concatenate([jnp.where(first_head, xp, zero), jnp.where(first_head, zero, xp)], axis=0)
            y_scr[:, ps] += _dot(lhs, rhs)

    for g in range(SSM_GROUPS):
        gs = slice(g * GROUP_COLS, (g + 1) * GROUP_COLS)
        zf = z_ref[:, gs].astype(f32)
        hg = y_scr[:, gs] * (zf * _sigmoid(zf))
        ms = jnp.mean(hg * hg, axis=-1, keepdims=True)
        y_ref[:, gs] = (hg * lax.rsqrt(ms + RMS_EPS) * ng_ref[:, gs]).astype(bf16)


def _ssd(z, xbc, dt, conv_w, conv_b, dt_bias, a_log, d_skip, norm_g, bsz, seq):
    n = bsz * seq
    L = SSM_CHUNK
    nc = seq // L
    pad = LANES - SSM_HEADS
    dtb = jnp.pad(dt_bias.astype(f32), (0, pad)).reshape(1, LANES)
    alog = jnp.pad(a_log.astype(f32), (0, pad)).reshape(1, LANES)
    dexp = jnp.repeat(d_skip.astype(f32), SSM_HEAD_DIM).reshape(1, D_INNER)
    head_of_col = jnp.arange(D_INNER, dtype=i32) // SSM_HEAD_DIM
    expand = (jnp.arange(LANES, dtype=i32)[:, None] == head_of_col[None, :]).astype(bf16)
    tril = jnp.tril(jnp.ones((L, L), f32)).astype(bf16)
    tok = lambda b, c: (b * nc + c, 0)
    const = lambda b, c: (0, 0)
    ex_rows = 2 * L + SUBLANES
    return pl.pallas_call(
        _ssd_body,
        grid=(bsz, nc),
        in_specs=[pl.BlockSpec((L, D_INNER), tok),
                  pl.BlockSpec((L, CONV_DIM), tok),
                  pl.BlockSpec((L, LANES), tok),
                  pl.BlockSpec((CONV_WIDTH, CONV_DIM), const),
                  pl.BlockSpec((1, CONV_DIM), const),
                  pl.BlockSpec((1, LANES), const),
                  pl.BlockSpec((1, LANES), const),
                  pl.BlockSpec((1, D_INNER), const),
                  pl.BlockSpec((1, D_INNER), const),
                  pl.BlockSpec((LANES, D_INNER), const),
                  pl.BlockSpec((L, L), const)],
        out_specs=pl.BlockSpec((L, D_INNER), tok),
        out_shape=jax.ShapeDtypeStruct((n, D_INNER), bf16),
        scratch_shapes=[pltpu.VMEM((L + 2 * _EXT_PAD, CONV_DIM), f32),
                        pltpu.VMEM((L, CONV_DIM), f32),
                        pltpu.VMEM((SSM_STATE, D_INNER), f32),
                        pltpu.VMEM((L, D_INNER), f32),
                        pltpu.VMEM((ex_rows, D_INNER), f32)],
        compiler_params=_params("parallel", "arbitrary"),
        name="ssm_ssd",
    )(z, xbc, dt, conv_w.astype(f32), conv_b.astype(f32).reshape(1, CONV_DIM), dtb, alog, dexp,
      norm_g.astype(f32).reshape(1, D_INNER), expand, tril)


def _mm_res_ln_body(y_ref, w_ref, h_ref, g_ref, b_ref, o_ref):
    mix = _dot(y_ref[...], w_ref[...])
    o_ref[...] = _layer_norm(DEEPNORM_ALPHA * h_ref[...] + mix, g_ref[...], b_ref[...])


def _mm_res_ln(y, w, h, g, b):
    n, k = y.shape
    tm = ROW_TILE
    const = lambda i: (0, 0)
    return pl.pallas_call(
        _mm_res_ln_body,
        grid=(n // tm,),
        in_specs=[pl.BlockSpec((tm, k), lambda i: (i, 0)),
                  pl.BlockSpec((k, D_MODEL), const),
                  pl.BlockSpec((tm, D_MODEL), lambda i: (i, 0)),
                  pl.BlockSpec((1, D_MODEL), const),
                  pl.BlockSpec((1, D_MODEL), const)],
        out_specs=pl.BlockSpec((tm, D_MODEL), lambda i: (i, 0)),
        out_shape=jax.ShapeDtypeStruct((n, D_MODEL), f32),
        compiler_params=_params("parallel"),
        name="out_proj_ln",
    )(y, w, h, g.astype(f32).reshape(1, D_MODEL), b.astype(f32).reshape(1, D_MODEL))


def _rope_tables(seq):
    half = ROT_DIM // 2
    inv_freq = ROPE_THETA ** (-jnp.arange(0, ROT_DIM, 2, dtype=f32) / ROT_DIM)
    ang = jnp.arange(seq, dtype=f32)[:, None] * inv_freq[None, :]
    cos, sin = jnp.cos(ang), jnp.sin(ang)
    j = jnp.arange(LANES) % ATTN_HEAD_DIM
    idx = j % half
    first = (j < half)[None, :]
    second = ((j >= half) & (j < ROT_DIM))[None, :]
    cos_t = jnp.where(first | second, cos[:, idx], 1.0)
    sin_a = jnp.where(first, -sin[:, idx], 0.0)
    sin_b = jnp.where(second, sin[:, idx], 0.0)
    return cos_t, sin_a, sin_b


def _rope_cols(x, cos_t, sin_a, sin_b):
    half = ROT_DIM // 2
    return x * cos_t + pltpu.roll(x, LANES - half, axis=1) * sin_a + pltpu.roll(x, half, axis=1) * sin_b


def _q_proj_body(h_ref, w_ref, cos_ref, sa_ref, sb_ref, q_ref):
    x = h_ref[...].astype(bf16)
    scale = ATTN_HEAD_DIM ** -0.5
    for c in range(ATTN_DIM // LANES):
        cs = slice(c * LANES, (c + 1) * LANES)
        q = _rope_cols(_dot(x, w_ref[:, cs]), cos_ref[...], sa_ref[...], sb_ref[...])
        q_ref[:, cs] = (q * scale).astype(bf16)


def _q_proj(h, w, tables, seq):
    n = h.shape[0]
    tm = ROW_TILE
    per_seq = seq // tm
    const = lambda i: (0, 0)
    pos = lambda i: (i % per_seq, 0)
    return pl.pallas_call(
        _q_proj_body,
        grid=(n // tm,),
        in_specs=[pl.BlockSpec((tm, D_MODEL), lambda i: (i, 0)),
                  pl.BlockSpec((D_MODEL, ATTN_DIM), const),
                  pl.BlockSpec((tm, LANES), pos), pl.BlockSpec((tm, LANES), pos), pl.BlockSpec((tm, LANES), pos)],
        out_specs=pl.BlockSpec((tm, ATTN_DIM), lambda i: (i, 0)),
        out_shape=jax.ShapeDtypeStruct((n, ATTN_DIM), bf16),
        compiler_params=_params("parallel"),
        name="attn_q_proj",
    )(h, w, *tables)


def _kv_proj_body(h_ref, wk_ref, wv_ref, cos_ref, sa_ref, sb_ref, k_ref, v_ref, km_ref):
    x = h_ref[...].astype(bf16)
    for c in range(ATTN_DIM // LANES):
        cs = slice(c * LANES, (c + 1) * LANES)
        k = _rope_cols(_dot(x, wk_ref[:, cs]), cos_ref[...], sa_ref[...], sb_ref[...])
        k_ref[:, cs] = k.astype(bf16)
        km_ref[0, :, cs] = jnp.mean(k, axis=0, keepdims=True)
        v_ref[:, cs] = _dot(x, wv_ref[:, cs]).astype(bf16)


def _kv_proj(h, wk, wv, tables, seq):
    n = h.shape[0]
    tm = MOBA_BLOCK
    per_seq = seq // tm
    const = lambda i: (0, 0)
    pos = lambda i: (i % per_seq, 0)
    return pl.pallas_call(
        _kv_proj_body,
        grid=(n // tm,),
        in_specs=[pl.BlockSpec((tm, D_MODEL), lambda i: (i, 0)),
                  pl.BlockSpec((D_MODEL, ATTN_DIM), const),
                  pl.BlockSpec((D_MODEL, ATTN_DIM), const),
                  pl.BlockSpec((tm, LANES), pos), pl.BlockSpec((tm, LANES), pos), pl.BlockSpec((tm, LANES), pos)],
        out_specs=[pl.BlockSpec((tm, ATTN_DIM), lambda i: (i, 0)),
                   pl.BlockSpec((tm, ATTN_DIM), lambda i: (i, 0)),
                   pl.BlockSpec((1, 1, ATTN_DIM), lambda i: (i, 0, 0))],
        out_shape=[jax.ShapeDtypeStruct((n, ATTN_DIM), bf16),
                   jax.ShapeDtypeStruct((n, ATTN_DIM), bf16),
                   jax.ShapeDtypeStruct((n // tm, 1, ATTN_DIM), f32)],
        compiler_params=_params("parallel"),
        name="attn_kv_proj",
    )(h, wk, wv, *tables)


def _moba_body(q_ref, k_ref, v_ref, km_ref, o_ref, *, nb):
    qb = pl.program_id(2)
    blk = MOBA_BLOCK
    q = q_ref[...]
    lane = lax.broadcasted_iota(i32, (1, LANES), 1)
    first_head = lane < ATTN_HEAD_DIM
    zero = jnp.zeros_like(q)
    q2 = jnp.concatenate([jnp.where(first_head, q, zero), jnp.where(first_head, zero, q)], axis=0)
    row = lax.broadcasted_iota(i32, (2 * blk, blk), 0)
    col = lax.broadcasted_iota(i32, (2 * blk, blk), 1)
    causal = (col <= row) & ((row < blk) | (col + blk <= row))
    eye = (lax.broadcasted_iota(i32, (blk, blk), 0) == lax.broadcasted_iota(i32, (blk, blk), 1)).astype(bf16)
    blk_id = lax.broadcasted_iota(i32, (nb, 2 * blk), 0)
    km_hi, km_lo = _split2(km_ref[...])
    past = blk_id < qb
    gate = jnp.where(past, _nt(km_hi, q2) + _nt(km_lo, q2), NEG_INF)
    rank = jnp.zeros((nb, 2 * blk), f32)
    for j2 in range(nb):
        gj = gate[j2:j2 + 1, :]
        beats = (gj > gate) | ((gj == gate) & (blk_id > j2))
        rank = rank + beats.astype(f32)
    sel_t = (past & (rank < MOBA_TOPK)).astype(bf16)
    sel = jnp.concatenate([_nt(eye, sel_t[:, 0:blk]), _nt(eye, sel_t[:, blk:2 * blk])], axis=0)

    for c in range(nb):
        @pl.when(qb == c)
        def _(c=c):
            w = (c + 1) * blk
            s = _nt(q2, k_ref[0:w, :])
            parts = [jnp.where(sel[:, j:j + 1] > 0.5, s[:, j * blk:(j + 1) * blk], NEG_INF) for j in range(c)]
            parts.append(jnp.where(causal, s[:, c * blk:w], NEG_INF))
            s = jnp.concatenate(parts, axis=1) if c else parts[0]
            m = jnp.max(s, axis=-1, keepdims=True)
            p = jnp.exp(s - m)
            l = jnp.sum(p, axis=-1, keepdims=True)
            o = _dot(p.astype(bf16), v_ref[0:w, :]) / l
            o_ref[...] = jnp.where(first_head, o[0:blk, :], o[blk:2 * blk, :]).astype(bf16)


def _moba(q, k, v, kmean, bsz, seq):
    n = bsz * seq
    nb = seq // MOBA_BLOCK
    n_pairs = ATTN_DIM // LANES
    return pl.pallas_call(
        functools.partial(_moba_body, nb=nb),
        grid=(bsz, n_pairs, nb),
        in_specs=[pl.BlockSpec((MOBA_BLOCK, LANES), lambda b, hp, qb: (b * nb + qb, hp)),
                  pl.BlockSpec((seq, LANES), lambda b, hp, qb: (b, hp)),
                  pl.BlockSpec((seq, LANES), lambda b, hp, qb: (b, hp)),
                  pl.BlockSpec((None, nb, LANES), lambda b, hp, qb: (b, 0, hp))],
        out_specs=pl.BlockSpec((MOBA_BLOCK, LANES), lambda b, hp, qb: (b * nb + qb, hp)),
        out_shape=jax.ShapeDtypeStruct((n, ATTN_DIM), bf16),
        compiler_params=_params("parallel", "parallel", "arbitrary"),
        name="moba_attention",
    )(q, k, v, kmean)


def _router_body(h_ref, wt_ref, b_ref, u_ref, idx_ref, gate_ref, rank_ref, cnt_ref, cnt_scr):
    i = pl.program_id(0)

    @pl.when(i == 0)
    def _():
        cnt_scr[...] = jnp.zeros_like(cnt_scr)

    tr = h_ref.shape[0]
    x_hi, x_lo = _split2(h_ref[...])
    w_hi, w_lo = _split2(wt_ref[...])
    logits = _nt(w_hi, x_hi) + _nt(w_hi, x_lo) + _nt(w_lo, x_hi) + b_ref[...]
    e_id = lax.broadcasted_iota(i32, (N_EXPERTS, tr), 0)
    cur = logits
    vals, idxs = [], []
    for _ in range(TOP_K):
        m = jnp.max(cur, axis=0, keepdims=True)
        idx = jnp.min(jnp.where(cur == m, e_id, N_EXPERTS), axis=0, keepdims=True)
        vals.append(m)
        idxs.append(idx)
        cur = jnp.where(e_id == idx, NEG_INF, cur)
    exps = [jnp.exp(v - vals[0]) for v in vals]
    denom = exps[0]
    for e in exps[1:]:
        denom = denom + e
    onehot = jnp.zeros((N_EXPERTS, tr), f32)
    for idx in idxs:
        onehot = onehot + (e_id == idx).astype(f32)
    base = cnt_scr[:, 0:1]
    before = _dot(onehot.astype(bf16), u_ref[...]) + base
    gate_ref[...] = jnp.zeros_like(gate_ref)
    for r in range(TOP_K):
        idx_ref[r:r + 1, :] = idxs[r]
        gate_ref[r:r + 1, :] = exps[r] / denom
        rank_ref[r:r + 1, :] = jnp.sum(jnp.where(e_id == idxs[r], before, 0.0), axis=0, keepdims=True).astype(i32)
    total = base + jnp.sum(onehot, axis=1, keepdims=True)
    cnt_scr[...] = jnp.broadcast_to(total, cnt_scr.shape)
    cnt_ref[...] = jnp.broadcast_to(total, cnt_ref.shape).astype(i32)


def _router(h, w_router, b_router):
    n = h.shape[0]
    tr = ROUTER_TILE
    wt = w_router.astype(f32).T
    b = b_router.astype(f32).reshape(N_EXPERTS, 1)
    strict_upper = (jnp.arange(tr)[:, None] < jnp.arange(tr)[None, :]).astype(bf16)
    const = lambda i: (0, 0)
    tok = lambda i: (0, i)
    return pl.pallas_call(
        _router_body,
        grid=(n // tr,),
        in_specs=[pl.BlockSpec((tr, D_MODEL), lambda i: (i, 0)),
                  pl.BlockSpec((N_EXPERTS, D_MODEL), const),
                  pl.BlockSpec((N_EXPERTS, 1), const),
                  pl.BlockSpec((tr, tr), const)],
        out_specs=[pl.BlockSpec((TOP_K, tr), tok),
                   pl.BlockSpec((SUBLANES, tr), tok),
                   pl.BlockSpec((TOP_K, tr), tok),
                   pl.BlockSpec((N_EXPERTS, LANES), const)],
        out_shape=[jax.ShapeDtypeStruct((TOP_K, n), i32),
                   jax.ShapeDtypeStruct((SUBLANES, n), f32),
                   jax.ShapeDtypeStruct((TOP_K, n), i32),
                   jax.ShapeDtypeStruct((N_EXPERTS, LANES), i32)],
        scratch_shapes=[pltpu.VMEM((N_EXPERTS, LANES), f32)],
        compiler_params=_params("arbitrary"),
        name="moe_router",
    )(h, wt, b, strict_upper)


def _gather_rows(idx_smem, src_hbm, dst_vmem, sem, n_rows, idx_base=0, both_queues=False):
    for r in range(n_rows):
        tok = idx_smem[idx_base + r]
        copy = pltpu.make_async_copy(src_hbm.at[pl.ds(tok, 1)], dst_vmem.at[pl.ds(r, 1)], sem)
        copy.start(priority=r % 2 if both_queues else 0)


def _wait_rows(src_hbm, dst_vmem, sem, n_rows):
    pltpu.make_async_copy(src_hbm.at[pl.ds(0, n_rows)], dst_vmem, sem).wait()


def _idx_copy(idx_hbm, idx_smem, idx_sem, tile, slot, n_idx):
    src = idx_hbm.at[pl.ds(pl.multiple_of(tile * n_idx, n_idx), n_idx)]
    dst = idx_smem.at[pl.ds(pl.multiple_of(slot * n_idx, n_idx), n_idx)]
    return pltpu.make_async_copy(src, dst, idx_sem.at[slot])


def _expert_body(blk_e_ref, nact_ref, first_ref, next_e_ref, tok_hbm, h_hbm, wgu_hbm, bgu_ref, wd_hbm, bd_ref,
                 y_ref, idx_smem, x_buf, wgu_st, wd_st, wgu_bf, wd_bf, idx_sem, row_sem, w_sem, *, layer):
    i = pl.program_id(0)
    tm = x_buf.shape[1]
    nact = nact_ref[0]
    last_blk = pl.num_programs(0) - 1
    slot = i % 2
    nxt = 1 - slot

    def weight_copies(e):
        return (pltpu.make_async_copy(wgu_hbm.at[layer, e], wgu_st, w_sem.at[0]),
                pltpu.make_async_copy(wd_hbm.at[layer, e], wd_st, w_sem.at[1]))

    @pl.when(i == 0)
    def _():
        for copy in weight_copies(blk_e_ref[0]):
            copy.start()
        first = _idx_copy(tok_hbm, idx_smem, idx_sem, 0, 0, tm)
        first.start()
        first.wait()
        _gather_rows(idx_smem, h_hbm, x_buf.at[0], row_sem.at[0], tm, both_queues=True)
        _idx_copy(tok_hbm, idx_smem, idx_sem, jnp.minimum(1, last_blk), 1, tm).start()

    @pl.when(i < nact)
    def _():
        @pl.when(i + 1 < nact)
        def _():
            _idx_copy(tok_hbm, idx_smem, idx_sem, jnp.minimum(i + 2, last_blk), slot, tm).start()

        _idx_copy(tok_hbm, idx_smem, idx_sem, jnp.minimum(i + 1, last_blk), nxt, tm).wait()
        _wait_rows(h_hbm, x_buf.at[slot], row_sem.at[slot], tm)
        _gather_rows(idx_smem, h_hbm, x_buf.at[nxt], row_sem.at[nxt], tm, idx_base=nxt * tm, both_queues=True)

        @pl.when(first_ref[i] == 1)
        def _():
            for copy in weight_copies(blk_e_ref[i]):
                copy.wait()
            for c in range(2 * D_EXPERT // 512):
                cs = slice(c * 512, (c + 1) * 512)
                wgu_bf[:, cs] = wgu_st[:, cs].astype(bf16)
            for c in range(D_MODEL // 512):
                cs = slice(c * 512, (c + 1) * 512)
                wd_bf[:, cs] = wd_st[:, cs].astype(bf16)

            @pl.when(next_e_ref[i] >= 0)
            def _():
                for copy in weight_copies(next_e_ref[i]):
                    copy.start()

        x = x_buf[slot].astype(bf16)
        gu = _dot(x, wgu_bf[...]) + bgu_ref[...]
        g_lin = jnp.minimum(gu[:, :D_EXPERT], SWIGLU_LIMIT)
        u_lin = jnp.clip(gu[:, D_EXPERT:], -SWIGLU_LIMIT, SWIGLU_LIMIT)
        act = (u_lin + 1.0) * (g_lin * _sigmoid(g_lin * SWIGLU_ALPHA))
        y_ref[...] = _dot(act.astype(bf16), wd_bf[...]) + bd_ref[...]

        @pl.when(i + 1 == nact)
        def _():
            _wait_rows(h_hbm, x_buf.at[nxt], row_sem.at[nxt], tm)

    @pl.when(i >= nact)
    def _():
        y_ref[...] = jnp.zeros_like(y_ref)


def _expert_ffn(h, row_tok, blk_e, nact, first, next_e, layer, w_gu, b_gu, w_down, b_down, n_blocks):
    tm = MOE_TM
    b_map = lambda i, be, na, fi, ne: (be[i], 0, 0)
    out_map = lambda i, be, na, fi, ne: (i, 0)
    return pl.pallas_call(
        functools.partial(_expert_body, layer=layer),
        grid_spec=pltpu.PrefetchScalarGridSpec(
            num_scalar_prefetch=4,
            grid=(n_blocks,),
            in_specs=[pl.BlockSpec(memory_space=pl.ANY),
                      pl.BlockSpec(memory_space=pl.ANY),
                      pl.BlockSpec(memory_space=pl.ANY),
                      pl.BlockSpec((None, 1, 2 * D_EXPERT), b_map),
                      pl.BlockSpec(memory_space=pl.ANY),
                      pl.BlockSpec((None, 1, D_MODEL), b_map)],
            out_specs=pl.BlockSpec((tm, D_MODEL), out_map),
            scratch_shapes=[pltpu.SMEM((2 * tm,), i32),
                            pltpu.VMEM((2, tm, D_MODEL), f32),
                            pltpu.VMEM((D_MODEL, 2 * D_EXPERT), f32),
                            pltpu.VMEM((D_EXPERT, D_MODEL), f32),
                            pltpu.VMEM((D_MODEL, 2 * D_EXPERT), bf16),
                            pltpu.VMEM((D_EXPERT, D_MODEL), bf16),
                            pltpu.SemaphoreType.DMA((2,)),
                            pltpu.SemaphoreType.DMA((2,)),
                            pltpu.SemaphoreType.DMA((2,))]),
        out_shape=jax.ShapeDtypeStruct((n_blocks * tm, D_MODEL), f32),
        compiler_params=_params("arbitrary"),
        name="moe_experts",
    )(blk_e, nact, first, next_e, row_tok, h, w_gu, b_gu, w_down, b_down)


def _combine_body(dest_hbm, ybuf_hbm, gate_ref, h_ref, g_ref, b_ref, o_ref, idx_smem, rows, idx_sem, row_sem):
    i = pl.program_id(0)
    tc = h_ref.shape[0]
    n_idx = TOP_K * tc
    last = pl.num_programs(0) - 1
    slot = i % 2
    nxt = 1 - slot

    def gather(s):
        for k in range(TOP_K):
            _gather_rows(idx_smem, ybuf_hbm, rows.at[s, k], row_sem.at[s], tc, idx_base=s * n_idx + k * tc,
                         both_queues=True)

    def wait(s):
        for k in range(TOP_K):
            _wait_rows(ybuf_hbm, rows.at[s, k], row_sem.at[s], tc)

    @pl.when(i == 0)
    def _():
        first = _idx_copy(dest_hbm, idx_smem, idx_sem, 0, 0, n_idx)
        first.start()
        first.wait()
        gather(0)
        _idx_copy(dest_hbm, idx_smem, idx_sem, jnp.minimum(1, last), 1, n_idx).start()

    @pl.when(i < last)
    def _():
        _idx_copy(dest_hbm, idx_smem, idx_sem, jnp.minimum(i + 2, last), slot, n_idx).start()

    _idx_copy(dest_hbm, idx_smem, idx_sem, jnp.minimum(i + 1, last), nxt, n_idx).wait()
    wait(slot)
    gather(nxt)
    gates = jnp.concatenate([gate_ref[...], jnp.zeros((tc - SUBLANES, tc), f32)], axis=0).T
    moe = rows[slot, 0] * gates[:, 0:1]
    for k in range(1, TOP_K):
        moe = moe + rows[slot, k] * gates[:, k:k + 1]
    o_ref[...] = _layer_norm(DEEPNORM_ALPHA * h_ref[...] + moe, g_ref[...], b_ref[...])

    @pl.when(i == last)
    def _():
        wait(nxt)


def _combine_ln(dest_tiles, ybuf, gates, h, g, b):
    n = h.shape[0]
    tc = COMBINE_TILE
    const = lambda i: (0, 0)
    return pl.pallas_call(
        _combine_body,
        grid=(n // tc,),
        in_specs=[pl.BlockSpec(memory_space=pl.ANY),
                  pl.BlockSpec(memory_space=pl.ANY),
                  pl.BlockSpec((SUBLANES, tc), lambda i: (0, i)),
                  pl.BlockSpec((tc, D_MODEL), lambda i: (i, 0)),
                  pl.BlockSpec((1, D_MODEL), const),
                  pl.BlockSpec((1, D_MODEL), const)],
        out_specs=pl.BlockSpec((tc, D_MODEL), lambda i: (i, 0)),
        out_shape=jax.ShapeDtypeStruct((n, D_MODEL), f32),
        scratch_shapes=[pltpu.SMEM((2 * TOP_K * tc,), i32),
                        pltpu.VMEM((2, TOP_K, tc, D_MODEL), f32),
                        pltpu.SemaphoreType.DMA((2,)),
                        pltpu.SemaphoreType.DMA((2,))],
        compiler_params=_params("arbitrary"),
        name="moe_combine_ln",
    )(dest_tiles, ybuf, gates, h, g.astype(f32).reshape(1, D_MODEL), b.astype(f32).reshape(1, D_MODEL))


def _moe_layer(h, layer, w_router, b_router, w_gu, b_gu, w_down, b_down, ln_g, ln_b):
    n = h.shape[0]
    tm = MOE_TM
    n_blocks = (n * TOP_K) // tm + N_EXPERTS
    idx, gates, rank, counts = _router(h, w_router, b_router)
    counts = counts[:, 0]
    blocks_per_e = (counts + tm - 1) // tm
    blk_end = jnp.cumsum(blocks_per_e)
    blk_start = blk_end - blocks_per_e
    nact = blk_end[-1:].astype(i32)
    e_ids = jnp.arange(N_EXPERTS, dtype=i32)
    start_of = jnp.sum(jnp.where(idx[:, :, None] == e_ids, blk_start, 0), axis=-1)
    dest = start_of * tm + rank
    blk_ids = jnp.minimum(jnp.arange(n_blocks, dtype=i32), nact[0] - 1)
    blk_e = jnp.minimum(jnp.sum(blk_end[None, :] <= blk_ids[:, None], axis=-1), N_EXPERTS - 1).astype(i32)
    tok_ids = jnp.broadcast_to(jnp.arange(n, dtype=i32)[None, :], (TOP_K, n))
    row_tok = jnp.zeros((n_blocks * tm,), i32).at[dest.reshape(-1)].set(tok_ids.reshape(-1), unique_indices=True)
    first = jnp.concatenate([jnp.ones((1,), i32), (blk_e[1:] != blk_e[:-1]).astype(i32)])
    group_end = jnp.sum(jnp.where(blk_e[:, None] == e_ids, blk_end, 0), axis=-1)
    next_e = jnp.where(group_end < nact[0], blk_e[jnp.minimum(group_end, n_blocks - 1)], -1).astype(i32)
    ybuf = _expert_ffn(h, row_tok, blk_e, nact, first, next_e, layer, w_gu,
                       b_gu.astype(f32).reshape(N_EXPERTS, 1, 2 * D_EXPERT),
                       w_down, b_down.astype(f32).reshape(N_EXPERTS, 1, D_MODEL), n_blocks)
    tc = COMBINE_TILE
    dest_tiles = dest.reshape(TOP_K, n // tc, tc).transpose(1, 0, 2).reshape(-1)
    return _combine_ln(dest_tiles, ybuf, gates, h, ln_g, ln_b)


def _mamba_layer(h, bsz, seq, w_in, conv_w, conv_b, dt_bias, a_log, d_skip, norm_g, w_out, ln_g, ln_b):
    wz = w_in[:, :D_INNER].astype(bf16)
    wx = w_in[:, D_INNER:D_INNER + CONV_DIM].astype(bf16)
    wdt = jnp.pad(w_in[:, D_INNER + CONV_DIM:], ((0, 0), (0, LANES - SSM_HEADS))).astype(bf16)
    z, xbc, dt = _in_proj(h, wz, wx, wdt)
    y = _ssd(z, xbc, dt, conv_w, conv_b, dt_bias, a_log, d_skip, norm_g, bsz, seq)
    return _mm_res_ln(y, w_out.astype(bf16), h, ln_g, ln_b)


def _moba_layer(h, bsz, seq, w_q, w_o, k, v, kmean, tables, ln_g, ln_b):
    q = _q_proj(h, w_q.astype(bf16), tables, seq)
    o = _moba(q, k, v, kmean, bsz, seq)
    return _mm_res_ln(o, w_o.astype(bf16), h, ln_g, ln_b)


def kernel(x, ssm_w_in, ssm_conv_w, ssm_conv_b, ssm_dt_bias, ssm_a_log, ssm_d, ssm_norm_g, ssm_w_out,
           kv_w_k, kv_w_v, attn_w_q, attn_w_o, moe_w_router, moe_b_router, moe_w_gate_up, moe_b_gate_up,
           moe_w_down, moe_b_down, ln_mix_g, ln_mix_b, ln_ffn_g, ln_ffn_b):
    bsz, seq, _ = x.shape
    h = x.reshape(bsz * seq, D_MODEL).astype(f32)
    tables = _rope_tables(seq)
    k = v = kmean = None
    for l in range(DEPTH):
        if l < N_A_LAYERS:
            h = _mamba_layer(h, bsz, seq, ssm_w_in[l], ssm_conv_w[l], ssm_conv_b[l], ssm_dt_bias[l], ssm_a_log[l],
                             ssm_d[l], ssm_norm_g[l], ssm_w_out[l], ln_mix_g[l], ln_mix_b[l])
        else:
            if l == N_A_LAYERS:
                k, v, kmean = _kv_proj(h, kv_w_k.astype(bf16), kv_w_v.astype(bf16), tables, seq)
                kmean = kmean.reshape(bsz, seq // MOBA_BLOCK, ATTN_DIM)
            j = l - N_A_LAYERS
            h = _moba_layer(h, bsz, seq, attn_w_q[j], attn_w_o[j], k, v, kmean, tables, ln_mix_g[l], ln_mix_b[l])
        h = _moe_layer(h, l, moe_w_router[l], moe_b_router[l], moe_w_gate_up, moe_b_gate_up[l],
                       moe_w_down, moe_b_down[l], ln_ffn_g[l], ln_ffn_b[l])
    return h.reshape(bsz, seq, D_MODEL).astype(x.dtype)
```
